```python
import math
import jax, jax.numpy as jnp
from jax import lax
import numpy as np

D_MODEL = 4096
BATCH = 1
SEQ = 16384
DEPTH = 2

N_MEM = 256
BRANCH_W = D_MODEL // 2
A_HEADS = BRANCH_W // 128
A_NOPE = 128
A_ROPE = 64
A_VDIM = 128
A_Q_RANK = D_MODEL // 4
A_KV_RANK = 512
ROPE_THETA = 10000.0
Q_BLOCK = 128
B_W = BRANCH_W
CONV_W = 31
C_HEADS = BRANCH_W // 512
C_DK = 256
C_DV = 512
C_CHUNK = 64
F_BIAS_INIT = 3.0
D_HEADS = 4
D_HDIM = 128
D_W = D_HEADS * D_HDIM
N_BRANCH = 4
EPS = 1e-6

IN_SIZES = (
    A_Q_RANK, A_KV_RANK, A_ROPE, A_HEADS * A_VDIM,
    2 * B_W, B_W,
    C_HEADS * C_DK, C_HEADS * C_DK, C_HEADS * C_DV,
    C_HEADS * C_DV, C_HEADS * C_DV, 4 * C_HEADS,
    D_W,
    N_BRANCH * D_MODEL,
)
IN_COLS = sum(IN_SIZES)
IN_SPLITS = tuple(int(s) for s in np.cumsum(IN_SIZES)[:-1])

kernel_name = "hybrid_mla_conformer_mlstm_encoder"


def rmsnorm(x, g):
    xf = x.astype(jnp.float32)
    y = xf * lax.rsqrt(jnp.mean(xf * xf, axis=-1, keepdims=True) + EPS)
    return y.astype(x.dtype) * g


def layernorm(x, g, b):
    xf = x.astype(jnp.float32)
    xc = xf - jnp.mean(xf, axis=-1, keepdims=True)
    y = xc * lax.rsqrt(jnp.mean(xc * xc, axis=-1, keepdims=True) + EPS)
    return y.astype(x.dtype) * g + b


def rope_tables(positions):
    inv_freq = ROPE_THETA ** (-jnp.arange(0, A_ROPE, 2, dtype=jnp.float32) / A_ROPE)
    ang = positions.astype(jnp.float32)[..., None] * inv_freq
    return jnp.cos(ang), jnp.sin(ang)


def apply_rope(x, cos, sin):
    xf = x.astype(jnp.float32)
    x1, x2 = jnp.split(xf, 2, axis=-1)
    return jnp.concatenate([x1 * cos - x2 * sin, x1 * sin + x2 * cos], axis=-1).astype(x.dtype)


def mla_attention(c_q, c_kv, k_rope_in, positions, q_norm_g, w_uq, kv_norm_g, w_ukv):
    B, S, _ = c_q.shape
    cos, sin = rope_tables(positions)
    q = (rmsnorm(c_q, q_norm_g) @ w_uq).reshape(B, S, A_HEADS, A_NOPE + A_ROPE)
    q_nope, q_rope = q[..., :A_NOPE], q[..., A_NOPE:]
    q_rope = apply_rope(q_rope, cos[:, :, None, :], sin[:, :, None, :])
    k_rope = apply_rope(k_rope_in, cos, sin)
    kv = (rmsnorm(c_kv, kv_norm_g) @ w_ukv).reshape(B, S, A_HEADS, A_NOPE + A_VDIM)
    k_nope, v = kv[..., :A_NOPE], kv[..., A_NOPE:]
    scale = (A_NOPE + A_ROPE) ** -0.5
    nb = S // Q_BLOCK

    def blocks(a):
        return jnp.moveaxis(a.reshape(B, nb, Q_BLOCK, *a.shape[2:]), 1, 0)

    def attend(qs):
        qn, qr = qs
        s = (jnp.einsum('bqhd,bkhd->bhqk', qn, k_nope)
             + jnp.einsum('bqhr,bkr->bhqk', qr, k_rope)).astype(jnp.float32) * scale
        p = jax.nn.softmax(s, axis=-1).astype(v.dtype)
        return jnp.einsum('bhqk,bkhd->bqhd', p, v)

    o = lax.map(attend, (blocks(q_nope), blocks(q_rope)))
    return jnp.moveaxis(o, 0, 1).reshape(B, S, A_HEADS * A_VDIM)


def conformer_conv(u_glu, conv_w, conv_b, ln_g, ln_b):
    a, g = jnp.split(u_glu, 2, axis=-1)
    u = a * jax.nn.sigmoid(g)
    u = lax.conv_general_dilated(u, conv_w[:, None, :], window_strides=(1,),
                                 padding=[(CONV_W // 2, CONV_W // 2)],
                                 dimension_numbers=('NWC', 'WIO', 'NWC'),
                                 feature_group_count=B_W) + conv_b
    return jax.nn.silu(layernorm(u, ln_g, ln_b))


def mlstm_chunkwise(q, k, v, i_pre, f_pre):
    B, S, H, _ = q.shape
    nc = S // C_CHUNK

    def chunks(a):
        a = a.astype(jnp.float32).reshape(B, nc, C_CHUNK, H, *a.shape[3:])
        return jnp.moveaxis(jnp.moveaxis(a, 1, 0), 3, 2)

    qc, kc, vc = chunks(q), chunks(k), chunks(v)
    ic = chunks(i_pre)
    lfc = jax.nn.log_sigmoid(chunks(f_pre))
    tri = jnp.tril(jnp.ones((C_CHUNK, C_CHUNK), dtype=bool))

    def step(carry, xs):
        Cs, ns, ms = carry
        qb, kb, vb, ib, lfb = xs
        bcum = jnp.cumsum(lfb, axis=-1)
        logw = jnp.where(tri, bcum[..., :, None] - bcum[..., None, :] + ib[..., None, :], -jnp.inf)
        inter = bcum + ms[..., None]
        m_loc = jnp.maximum(inter, jnp.max(logw, axis=-1))
        w_intra = jnp.exp(logw - m_loc[..., None])
        w_inter = jnp.exp(inter - m_loc)
        s = jnp.einsum('bhjd,bhsd->bhjs', qb, kb) * w_intra
        num = (jnp.einsum('bhjs,bhsv->bhjv', s, vb)
               + w_inter[..., None] * jnp.einsum('bhjd,bhdv->bhjv', qb, Cs))
        den = jnp.sum(s, axis=-1) + w_inter * jnp.einsum('bhjd,bhd->bhj', qb, ns)
        h = num / jnp.maximum(jnp.abs(den), jnp.exp(-m_loc))[..., None]
        b_tot = bcum[..., -1]
        logw_end = b_tot[..., None] - bcum + ib
        m_new = jnp.maximum(b_tot + ms, jnp.max(logw_end, axis=-1))
        decay = jnp.exp(b_tot + ms - m_new)
        wk = jnp.exp(logw_end - m_new[..., None])[..., None] * kb
        C_new = decay[..., None, None] * Cs + jnp.einsum('bhsd,bhsv->bhdv', wk, vb)
        n_new = decay[..., None] * ns + jnp.sum(wk, axis=2)
        return (C_new, n_new, m_new), h

    init = (jnp.zeros((B, H, C_DK, C_DV), jnp.float32),
            jnp.zeros((B, H, C_DK), jnp.float32),
            jnp.zeros((B, H), jnp.float32))
    _, hs = lax.scan(step, init, (qc, kc, vc, ic, lfc))
    return jnp.moveaxis(jnp.moveaxis(hs, 2, 3), 0, 1).reshape(B, S, H, C_DV)


def mlstm_branch(q, k, v, o_pre, if_pre, gate_b, norm_g):
    B, S, _ = q.shape
    q = q.reshape(B, S, C_HEADS, C_DK)
    k = k.reshape(B, S, C_HEADS, C_DK) * (C_DK ** -0.5)
    v = v.reshape(B, S, C_HEADS, C_DV)
    g = (if_pre.reshape(B, S, 2, 2, C_HEADS) + gate_b).astype(jnp.float32)
    h_fwd = mlstm_chunkwise(q, k, v, g[:, :, 0, 0], g[:, :, 0, 1])
    flip = lambda a: jnp.flip(a, axis=1)
    h_bwd = flip(mlstm_chunkwise(flip(q), flip(k), flip(v), flip(g[:, :, 1, 0]), flip(g[:, :, 1, 1])))
    h = h_fwd + h_bwd
    h = h * lax.rsqrt(jnp.mean(h * h, axis=-1, keepdims=True) + EPS)
    h = h.reshape(B, S, C_HEADS * C_DV) * norm_g
    return (jax.nn.sigmoid(o_pre.astype(jnp.float32)) * h).astype(v.dtype)


def memory_attention(q, mem, mem_norm_g, w_mem_kv):
    B, S, _ = q.shape
    M = mem.shape[1]
    kv = rmsnorm(mem, mem_norm_g) @ w_mem_kv
    k = kv[..., :D_W].reshape(B, M, D_HEADS, D_HDIM)
    v = kv[..., D_W:].reshape(B, M, D_HEADS, D_HDIM)
    qh = q.reshape(B, S, D_HEADS, D_HDIM)
    s = jnp.einsum('bshd,bmhd->bhsm', qh, k).astype(jnp.float32) * (D_HDIM ** -0.5)
    p = jax.nn.softmax(s, axis=-1).astype(v.dtype)
    return jnp.einsum('bhsm,bmhd->bshd', p, v).reshape(B, S, D_W)


def setup_inputs(seed: int = 0) -> dict:
    key = jax.random.key(seed)
    ks = jax.random.split(key, 24)
    f32 = jnp.float32
    nrm = lambda k, shape, scale: jax.random.normal(k, shape, f32) * scale
    gain = lambda k, shape: 1.0 + 0.05 * jax.random.normal(k, shape, f32)
    gate_offset = jnp.array([0.0, F_BIAS_INIT], f32)[None, None, :, None]
    return {
        "x": nrm(ks[0], (BATCH, SEQ, D_MODEL), 1.0),
        "mem": nrm(ks[1], (BATCH, N_MEM, D_MODEL), 1.0),
        "positions": jnp.broadcast_to(jnp.arange(SEQ, dtype=jnp.int32)[None, :], (BATCH, SEQ)),
        "norm_g": gain(ks[2], (DEPTH, D_MODEL)),
        "w_in": nrm(ks[3], (DEPTH, D_MODEL, IN_COLS), D_MODEL ** -0.5),
        "mla_cq_norm_g": gain(ks[4], (DEPTH, A_Q_RANK)),
        "w_uq": nrm(ks[5], (DEPTH, A_Q_RANK, A_HEADS * (A_NOPE + A_ROPE)), A_Q_RANK ** -0.5),
        "mla_ckv_norm_g": gain(ks[6], (DEPTH, A_KV_RANK)),
        "w_ukv": nrm(ks[7], (DEPTH, A_KV_RANK, A_HEADS * (A_NOPE + A_VDIM)), A_KV_RANK ** -0.5),
        "conv_w": nrm(ks[8], (DEPTH, CONV_W, B_W), CONV_W ** -0.5),
        "conv_b": nrm(ks[9], (DEPTH, B_W), 0.02),
        "conv_ln_g": gain(ks[10], (DEPTH, B_W)),
        "conv_ln_b": nrm(ks[11], (DEPTH, B_W), 0.02),
        "mlstm_gate_b": gate_offset + nrm(ks[12], (DEPTH, 2, 2, C_HEADS), 0.1),
        "mlstm_norm_g": gain(ks[13], (DEPTH, C_HEADS * C_DV)),
        "mem_norm_g": gain(ks[14], (DEPTH, D_MODEL)),
        "w_mem_kv": nrm(ks[15], (DEPTH, D_MODEL, 2 * D_W), D_MODEL ** -0.5),
        "w_proj_a": nrm(ks[16], (DEPTH, A_HEADS * A_VDIM, D_MODEL), (A_HEADS * A_VDIM) ** -0.5),
        "w_proj_b": nrm(ks[17], (DEPTH, B_W, D_MODEL), B_W ** -0.5),
        "w_proj_c": nrm(ks[18], (DEPTH, C_HEADS * C_DV, D_MODEL), (C_HEADS * C_DV) ** -0.5),
        "w_proj_d": nrm(ks[19], (DEPTH, D_W, D_MODEL), D_W ** -0.5),
        "w_out": nrm(ks[20], (DEPTH, D_MODEL, D_MODEL), D_MODEL ** -0.5),
        "final_norm_g": gain(ks[21], (D_MODEL,)),
    }


def reference(x, mem, positions, norm_g, w_in, mla_cq_norm_g, w_uq, mla_ckv_norm_g, w_ukv,
              conv_w, conv_b, conv_ln_g, conv_ln_b, mlstm_gate_b, mlstm_norm_g,
              mem_norm_g, w_mem_kv, w_proj_a, w_proj_b, w_proj_c, w_proj_d, w_out, final_norm_g):
    B, S, _ = x.shape
    for l in range(DEPTH):
        h = rmsnorm(x, norm_g[l])
        proj = h @ w_in[l]
        (a_cq, a_ckv, a_kr, a_gate, b_glu, b_gate, c_q, c_k, c_v, c_o, c_gate, c_if,
         d_q, merge) = jnp.split(proj, list(IN_SPLITS), axis=-1)
        y_a = mla_attention(a_cq, a_ckv, a_kr, positions, mla_cq_norm_g[l], w_uq[l],
                            mla_ckv_norm_g[l], w_ukv[l]) * jax.nn.silu(a_gate)
        y_b = conformer_conv(b_glu, conv_w[l], conv_b[l], conv_ln_g[l], conv_ln_b[l]) * jax.nn.silu(b_gate)
        y_c = mlstm_branch(c_q, c_k, c_v, c_o, c_if, mlstm_gate_b[l], mlstm_norm_g[l]) * jax.nn.silu(c_gate)
        y_d = memory_attention(d_q, mem, mem_norm_g[l], w_mem_kv[l])
        gates = jax.nn.sigmoid(merge.astype(jnp.float32)).astype(x.dtype).reshape(B, S, N_BRANCH, D_MODEL)
        z = (gates[:, :, 0] * (y_a @ w_proj_a[l])
             + gates[:, :, 1] * (y_b @ w_proj_b[l])
             + gates[:, :, 2] * (y_c @ w_proj_c[l])
             + gates[:, :, 3] * (y_d @ w_proj_d[l]))
        x = x + z @ w_out[l]
    return rmsnorm(x, final_norm_g)
```

```python
import functools

import jax
import jax.numpy as jnp
from jax import lax
from jax.experimental import pallas as pl
from jax.experimental.pallas import tpu as pltpu

F32 = jnp.float32
BF16 = jnp.bfloat16

D_MODEL = 4096
BRANCH_W = D_MODEL // 2
A_HEADS = 16
A_NOPE = 128
A_ROPE = 64
A_VDIM = 128
A_Q_RANK = 1024
A_KV_RANK = 512
ROPE_THETA = 10000.0
CONV_W = 31
C_HEADS = 4
C_DK = 256
C_DV = 512
D_HEADS = 4
D_HDIM = 128
D_W = D_HEADS * D_HDIM
N_BRANCH = 4
EPS = 1e-6

LANES = 128
A_QK = 2 * LANES
MLSTM_CHUNK = 256
CONV_HALO = 16
VMEM_LIMIT = 56 * 1024 * 1024

P1_CQ, P1_CKV, P1_DQ, P1_KR, P1_KRROT, P1_CIF, P1_W = 0, 1024, 1536, 2048, 2176, 2304, 2560
P2_AGATE, P2_GLU_A, P2_GLU_G, P2_BGATE = 0, 2048, 4096, 6144
P2_CQ, P2_CK, P2_CV, P2_CO, P2_CGATE, P2_W = 8192, 9216, 10240, 12288, 14336, 16384
IN_AGATE, IN_CIF, IN_DQ, IN_MERGE = 1600, 17984, 18000, 18512


def _params(semantics):
    return pltpu.CompilerParams(dimension_semantics=semantics, vmem_limit_bytes=VMEM_LIMIT)


def _silu(x):
    return x * jax.nn.sigmoid(x)


def _rms(x, g):
    return x * lax.rsqrt(jnp.mean(x * x, axis=-1, keepdims=True) + EPS) * g


def _rmsnorm_kernel(x_ref, g_ref, o_ref):
    o_ref[...] = _rms(x_ref[...], g_ref[...]).astype(o_ref.dtype)


def _rmsnorm(x, g, out_dtype, tm=256):
    m, d = x.shape
    tm = min(tm, m)
    return pl.pallas_call(
        _rmsnorm_kernel,
        grid=(m // tm,),
        in_specs=[pl.BlockSpec((tm, d), lambda i: (i, 0)), pl.BlockSpec((1, d), lambda i: (0, 0))],
        out_specs=pl.BlockSpec((tm, d), lambda i: (i, 0)),
        out_shape=jax.ShapeDtypeStruct((m, d), out_dtype),
        compiler_params=_params(("parallel",)),
        name="rmsnorm",
    )(x, g.reshape(1, d))


def _mm_kernel(a_ref, w_ref, o_ref):
    o_ref[...] = jnp.dot(a_ref[...], w_ref[...], preferred_element_type=F32).astype(o_ref.dtype)


def _mm_res_kernel(a_ref, w_ref, r_ref, o_ref):
    acc = jnp.dot(a_ref[...], w_ref[...], preferred_element_type=F32)
    o_ref[...] = (r_ref[...] + acc).astype(o_ref.dtype)


def _matmul(a, w, out_dtype, *, tm, tn, residual=None, name="matmul"):
    m, k = a.shape
    n = w.shape[1]
    tm, tn = min(tm, m), min(tn, n)
    in_specs = [pl.BlockSpec((tm, k), lambda j, i: (i, 0)), pl.BlockSpec((k, tn), lambda j, i: (0, j))]
    args = [a, w]
    kern = _mm_kernel
    if residual is not None:
        in_specs.append(pl.BlockSpec((tm, tn), lambda j, i: (i, j)))
        args.append(residual)
        kern = _mm_res_kernel
    return pl.pallas_call(
        kern,
        grid=(n // tn, m // tm),
        in_specs=in_specs,
        out_specs=pl.BlockSpec((tm, tn), lambda j, i: (i, j)),
        out_shape=jax.ShapeDtypeStruct((m, n), out_dtype),
        compiler_params=_params(("parallel", "parallel")),
        name=name,
    )(*args)


def _mla_prep_kernel(cq_ref, ckv_ref, kr_ref, krrot_ref, pos_ref, invf_ref, gq_ref, gkv_ref,
                     wq_ref, wkv_ref, q_ref, k_ref, v_ref, cnq_s, cnkv_s, krope_s, cos_s, sin_s):
    @pl.when(pl.program_id(1) == 0)
    def _():
        ang = pos_ref[...] * invf_ref[...]
        cos, sin = jnp.cos(ang), jnp.sin(ang)
        cos_s[...] = cos
        sin_s[...] = sin
        cnq_s[...] = _rms(cq_ref[...], gq_ref[...]).astype(BF16)
        cnkv_s[...] = _rms(ckv_ref[...], gkv_ref[...]).astype(BF16)
        krope_s[...] = (kr_ref[...] * cos + krrot_ref[...] * sin).astype(BF16)

    scale = (A_NOPE + A_ROPE) ** -0.5
    qh = jnp.dot(cnq_s[...], wq_ref[0], preferred_element_type=F32)
    q_rope = qh[:, LANES:2 * LANES] * cos_s[...] + qh[:, 2 * LANES:] * sin_s[...]
    q_ref[0, :, :LANES] = (qh[:, :LANES] * scale).astype(BF16)
    q_ref[0, :, LANES:] = (q_rope * scale).astype(BF16)
    kvh = jnp.dot(cnkv_s[...], wkv_ref[0], preferred_element_type=F32)
    k_ref[0, :, :LANES] = kvh[:, :LANES].astype(BF16)
    k_ref[0, :, LANES:] = krope_s[...]
    v_ref[0] = kvh[:, LANES:].astype(BF16)


def _mla_prep(p1, pos, invf, gq, gkv, wq, wkv, tm=512):
    s = p1.shape[0]
    tm = min(tm, s)
    row = lambda w, c: pl.BlockSpec((tm, w), lambda i, h, c=c: (i, c))
    const = lambda shape: pl.BlockSpec(shape, lambda i, h: (0,) * len(shape))
    head_out = lambda w: pl.BlockSpec((1, tm, w), lambda i, h: (h, i, 0))
    return pl.pallas_call(
        _mla_prep_kernel,
        grid=(s // tm, A_HEADS),
        in_specs=[
            row(A_Q_RANK, P1_CQ // A_Q_RANK), row(A_KV_RANK, P1_CKV // A_KV_RANK),
            row(LANES, P1_KR // LANES), row(LANES, P1_KRROT // LANES),
            pl.BlockSpec((tm, 1), lambda i, h: (i, 0)), const((1, LANES)),
            const((1, A_Q_RANK)), const((1, A_KV_RANK)),
            pl.BlockSpec((1, A_Q_RANK, 3 * LANES), lambda i, h: (h, 0, 0)),
            pl.BlockSpec((1, A_KV_RANK, 2 * LANES), lambda i, h: (h, 0, 0)),
        ],
        out_specs=[head_out(A_QK), head_out(A_QK), head_out(A_VDIM)],
        out_shape=[jax.ShapeDtypeStruct((A_HEADS, s, A_QK), BF16),
                   jax.ShapeDtypeStruct((A_HEADS, s, A_QK), BF16),
                   jax.ShapeDtypeStruct((A_HEADS, s, A_VDIM), BF16)],
        scratch_shapes=[pltpu.VMEM((tm, A_Q_RANK), BF16), pltpu.VMEM((tm, A_KV_RANK), BF16),
                        pltpu.VMEM((tm, LANES), BF16), pltpu.VMEM((tm, LANES), F32),
                        pltpu.VMEM((tm, LANES), F32)],
        compiler_params=_params(("parallel", "arbitrary")),
        name="mla_prep",
    )(p1, p1, p1, p1, pos, invf, gq, gkv, wq, wkv)


def _attn_kernel(q_ref, k_ref, v_ref, gate_ref, o_ref, m_s, l_s, acc_s):
    kv = pl.program_id(2)

    @pl.when(kv == 0)
    def _():
        m_s[...] = jnp.full(m_s.shape, -jnp.inf, F32)
        l_s[...] = jnp.zeros(l_s.shape, F32)
        acc_s[...] = jnp.zeros(acc_s.shape, F32)

    s = lax.dot_general(q_ref[0], k_ref[0], (((1,), (1,)), ((), ())), preferred_element_type=F32)
    m_prev = m_s[...]
    m_new = jnp.maximum(m_prev, jnp.max(s, axis=-1, keepdims=True))
    alpha = jnp.exp(m_prev - m_new)
    p = jnp.exp(s - m_new)
    l_s[...] = alpha * l_s[...] + jnp.sum(p, axis=-1, keepdims=True)
    acc_s[...] = alpha * acc_s[...] + jnp.dot(p.astype(BF16), v_ref[0], preferred_element_type=F32)
    m_s[...] = m_new

    @pl.when(kv == pl.num_programs(2) - 1)
    def _():
        o = acc_s[...] / l_s[...]
        o_ref[...] = (o * _silu(gate_ref[...].astype(F32))).astype(o_ref.dtype)


def _attention(q, k, v, p2, tq=1024, tk=1024):
    h, s, _ = q.shape
    tq, tk = min(tq, s), min(tk, s)
    return pl.pallas_call(
        _attn_kernel,
        grid=(h, s // tq, s // tk),
        in_specs=[
            pl.BlockSpec((1, tq, A_QK), lambda h, i, j: (h, i, 0)),
            pl.BlockSpec((1, tk, A_QK), lambda h, i, j: (h, j, 0)),
            pl.BlockSpec((1, tk, A_VDIM), lambda h, i, j: (h, j, 0)),
            pl.BlockSpec((tq, A_VDIM), lambda h, i, j: (i, P2_AGATE // A_VDIM + h)),
        ],
        out_specs=pl.BlockSpec((tq, A_VDIM), lambda h, i, j: (i, h)),
        out_shape=jax.ShapeDtypeStruct((s, A_HEADS * A_VDIM), BF16),
        scratch_shapes=[pltpu.VMEM((tq, 1), F32), pltpu.VMEM((tq, 1), F32), pltpu.VMEM((tq, A_VDIM), F32)],
        compiler_params=_params(("parallel", "parallel", "arbitrary")),
        name="mla_attention",
    )(q, k, v, p2)


CONV_CW = 128


def _conv_kernel(a_ref, g_ref, ap_ref, gp_ref, an_ref, gn_ref, bg_ref, w_ref, cb_ref, lg_ref, lb_ref,
                 o_ref, u_s, c_s):
    i = pl.program_id(0)
    tm = a_ref.shape[0]
    glu = lambda a, g: a.astype(F32) * jax.nn.sigmoid(g.astype(F32))
    u_s[CONV_HALO:CONV_HALO + tm, :] = glu(a_ref[...], g_ref[...])
    prev = glu(ap_ref[...], gp_ref[...])
    nxt = glu(an_ref[...], gn_ref[...])
    u_s[:CONV_HALO, :] = jnp.where(i > 0, prev, 0.0)
    u_s[CONV_HALO + tm:, :] = jnp.where(i < pl.num_programs(0) - 1, nxt, 0.0)
    off = CONV_HALO - CONV_W // 2
    for c in range(0, BRANCH_W, CONV_CW):
        acc = jnp.broadcast_to(cb_ref[:, c:c + CONV_CW], (tm, CONV_CW))
        for t in range(CONV_W):
            acc = acc + u_s[off + t:off + t + tm, c:c + CONV_CW] * w_ref[t:t + 1, c:c + CONV_CW]
        c_s[:, c:c + CONV_CW] = acc
    x = c_s[...]
    xc = x - jnp.mean(x, axis=-1, keepdims=True)
    y = xc * lax.rsqrt(jnp.mean(xc * xc, axis=-1, keepdims=True) + EPS) * lg_ref[...] + lb_ref[...]
    o_ref[...] = (_silu(y) * _silu(bg_ref[...].astype(F32))).astype(o_ref.dtype)


def _conv_module(p2, w, cb, lg, lb, tm=256):
    s = p2.shape[0]
    tm = min(tm, s)
    r = tm // CONV_HALO
    nh = s // CONV_HALO
    wide = lambda c: pl.BlockSpec((tm, BRANCH_W), lambda i, c=c: (i, c))
    prev = lambda c: pl.BlockSpec((CONV_HALO, BRANCH_W), lambda i, c=c: (jnp.maximum(i * r - 1, 0), c))
    nxt = lambda c: pl.BlockSpec((CONV_HALO, BRANCH_W), lambda i, c=c: (jnp.minimum((i + 1) * r, nh - 1), c))
    const = lambda rows: pl.BlockSpec((rows, BRANCH_W), lambda i: (0, 0))
    ca, cg, cbg = P2_GLU_A // BRANCH_W, P2_GLU_G // BRANCH_W, P2_BGATE // BRANCH_W
    return pl.pallas_call(
        _conv_kernel,
        grid=(s // tm,),
        in_specs=[wide(ca), wide(cg), prev(ca), prev(cg), nxt(ca), nxt(cg), wide(cbg),
                  const(CONV_W + 1), const(1), const(1), const(1)],
        out_specs=pl.BlockSpec((tm, BRANCH_W), lambda i: (i, 0)),
        out_shape=jax.ShapeDtypeStruct((s, BRANCH_W), BF16),
        scratch_shapes=[pltpu.VMEM((tm + 2 * CONV_HALO, BRANCH_W), F32), pltpu.VMEM((tm, BRANCH_W), F32)],
        compiler_params=_params(("parallel",)),
        name="conv_module",
    )(p2, p2, p2, p2, p2, p2, p2, w, cb, lg, lb)


def _mlstm_kernel(qf_ref, kf_ref, vf_ref, gf_ref, qb_ref, kb_ref, vb_ref, gb_ref, bias_ref,
                  hf_ref, hb_ref, c_s, n_s, m_s):
    L = qf_ref.shape[0]

    @pl.when(pl.program_id(0) == 0)
    def _():
        c_s[...] = jnp.zeros(c_s.shape, F32)
        n_s[...] = jnp.zeros(n_s.shape, F32)
        m_s[...] = jnp.zeros(m_s.shape, F32)

    rows = lax.broadcasted_iota(jnp.int32, (L, L), 0)
    cols = lax.broadcasted_iota(jnp.int32, (L, L), 1)
    lower = rows >= cols
    upper = rows <= cols
    tri_l = lower.astype(F32)
    tri_u = upper.astype(F32)
    hi = lax.Precision.HIGHEST

    for d, (q_ref, k_ref, v_ref, g_ref, h_ref) in enumerate(
            ((qf_ref, kf_ref, vf_ref, gf_ref, hf_ref), (qb_ref, kb_ref, vb_ref, gb_ref, hb_ref))):
        g = g_ref[...] + bias_ref[...]
        g_t = g.T
        lf = jax.nn.log_sigmoid(g)
        lf_t = jax.nn.log_sigmoid(g_t)
        mask = lower if d == 0 else upper
        bcum = jnp.dot(tri_l if d == 0 else tri_u, lf, precision=hi, preferred_element_type=F32)
        bcum_t = jnp.dot(lf_t, tri_u if d == 0 else tri_l, precision=hi, preferred_element_type=F32)
        last = L - 1 if d == 0 else 0
        for h in range(C_HEADS):
            ci, cf, sidx = d * 8 + h, d * 8 + 4 + h, d * C_HEADS + h
            qh = q_ref[:, h * C_DK:(h + 1) * C_DK]
            kh = k_ref[:, h * C_DK:(h + 1) * C_DK] * (C_DK ** -0.5)
            vh = v_ref[:, h * C_DV:(h + 1) * C_DV]
            bcol, brow = bcum[:, cf:cf + 1], bcum_t[cf:cf + 1, :]
            icol, irow = g[:, ci:ci + 1], g_t[ci:ci + 1, :]
            m_prev = m_s[sidx]
            c_prev = c_s[sidx]
            n_prev = n_s[sidx]

            logw = jnp.where(mask, bcol - brow + irow, -jnp.inf)
            inter = bcol + m_prev
            m_loc = jnp.maximum(inter, jnp.max(logw, axis=-1, keepdims=True))
            w_intra = jnp.exp(logw - m_loc)
            w_inter = jnp.exp(inter - m_loc)
            s = lax.dot_general(qh, kh, (((1,), (1,)), ((), ())), preferred_element_type=F32) * w_intra
            num = (jnp.dot(s.astype(BF16), vh, preferred_element_type=F32)
                   + w_inter * jnp.dot(qh, c_prev.astype(BF16), preferred_element_type=F32))
            den = (jnp.sum(s, axis=-1, keepdims=True)
                   + w_inter * jnp.sum(qh.astype(F32) * n_prev, axis=-1, keepdims=True))
            h_ref[:, h * C_DV:(h + 1) * C_DV] = num / jnp.maximum(jnp.abs(den), jnp.exp(-m_loc))

            b_tot = bcol[last:last + 1, :]
            lw_end = b_tot - bcol + icol
            m_new = jnp.maximum(b_tot + m_prev, jnp.max(lw_end, axis=0, keepdims=True))
            decay = jnp.exp(b_tot + m_prev - m_new)
            wk = jnp.exp(lw_end - m_new) * kh.astype(F32)
            c_s[sidx] = decay * c_prev + lax.dot_general(
                wk.astype(BF16), vh, (((0,), (0,)), ((), ())), preferred_element_type=F32)
            n_s[sidx] = decay * n_prev + jnp.sum(wk, axis=0, keepdims=True)
            m_s[sidx] = m_new


def _mlstm(p1, p2, bias):
    s = p2.shape[0]
    L = min(MLSTM_CHUNK, s)
    nc = s // L
    wq, wv = C_HEADS * C_DK, C_HEADS * C_DV
    fwd = lambda w, c: pl.BlockSpec((L, w), lambda i, c=c: (i, c))
    bwd = lambda w, c: pl.BlockSpec((L, w), lambda i, c=c: (nc - 1 - i, c))
    specs = lambda mk: [mk(wq, P2_CQ // wq), mk(wq, P2_CK // wq), mk(wv, P2_CV // wv)]
    return pl.pallas_call(
        _mlstm_kernel,
        grid=(nc,),
        in_specs=specs(fwd) + [fwd(LANES, P1_CIF // LANES)] + specs(bwd) + [bwd(LANES, P1_CIF // LANES)]
        + [pl.BlockSpec((1, LANES), lambda i: (0, 0))],
        out_specs=[pl.BlockSpec((L, wv), lambda i: (i, 0)), pl.BlockSpec((L, wv), lambda i: (nc - 1 - i, 0))],
        out_shape=[jax.ShapeDtypeStruct((s, wv), F32), jax.ShapeDtypeStruct((s, wv), F32)],
        scratch_shapes=[pltpu.VMEM((2 * C_HEADS, C_DK, C_DV), F32), pltpu.VMEM((2 * C_HEADS, 1, C_DK), F32),
                        pltpu.VMEM((2 * C_HEADS, 1, 1), F32)],
        compiler_params=_params(("arbitrary",)),
        name="mlstm_scan",
    )(p2, p2, p2, p1, p2, p2, p2, p1, bias)


def _mlstm_out_kernel(hf_ref, hb_ref, o_ref, gate_ref, ng_ref, y_ref):
    for h in range(C_HEADS):
        sl = slice(h * C_DV, (h + 1) * C_DV)
        x = hf_ref[:, sl] + hb_ref[:, sl]
        x = x * lax.rsqrt(jnp.mean(x * x, axis=-1, keepdims=True) + EPS) * ng_ref[:, sl]
        x = jax.nn.sigmoid(o_ref[:, sl].astype(F32)) * x
        y_ref[:, sl] = (x * _silu(gate_ref[:, sl].astype(F32))).astype(y_ref.dtype)


def _mlstm_out(hf, hb, p2, ng, tm=256):
    s, w = hf.shape
    tm = min(tm, s)
    row = lambda c: pl.BlockSpec((tm, w), lambda i, c=c: (i, c))
    return pl.pallas_call(
        _mlstm_out_kernel,
        grid=(s // tm,),
        in_specs=[row(0), row(0), row(P2_CO // w), row(P2_CGATE // w), pl.BlockSpec((1, w), lambda i: (0, 0))],
        out_specs=row(0),
        out_shape=jax.ShapeDtypeStruct((s, w), BF16),
        compiler_params=_params(("parallel",)),
        name="mlstm_out",
    )(hf, hb, p2, p2, ng)


def _mem_kv_kernel(mem_ref, g_ref, w_ref, o_ref):
    hn = _rms(mem_ref[...], g_ref[...]).astype(BF16)
    o_ref[...] = jnp.dot(hn, w_ref[...], preferred_element_type=F32).astype(o_ref.dtype)


def _mem_kv(mem, g, w):
    nm, d = mem.shape
    n = w.shape[1]
    tn = 512
    return pl.pallas_call(
        _mem_kv_kernel,
        grid=(n // tn,),
        in_specs=[pl.BlockSpec((nm, d), lambda j: (0, 0)), pl.BlockSpec((1, d), lambda j: (0, 0)),
                  pl.BlockSpec((d, tn), lambda j: (0, j))],
        out_specs=pl.BlockSpec((nm, tn), lambda j: (0, j)),
        out_shape=jax.ShapeDtypeStruct((nm, n), BF16),
        compiler_params=_params(("parallel",)),
        name="mem_kv",
    )(mem, g, w)


def _mem_attn_kernel(q_ref, kv_ref, o_ref):
    scale = D_HDIM ** -0.5
    for h in range(D_HEADS):
        sl = slice(h * D_HDIM, (h + 1) * D_HDIM)
        q = (q_ref[:, sl] * scale).astype(BF16)
        s = lax.dot_general(q, kv_ref[:, sl], (((1,), (1,)), ((), ())), preferred_element_type=F32)
        p = jnp.exp(s - jnp.max(s, axis=-1, keepdims=True))
        l = jnp.sum(p, axis=-1, keepdims=True)
        v = kv_ref[:, D_W + h * D_HDIM:D_W + (h + 1) * D_HDIM]
        o_ref[:, sl] = (jnp.dot(p.astype(BF16), v, preferred_element_type=F32) / l).astype(o_ref.dtype)


def _mem_attention(p1, kv, tm=512):
    s = p1.shape[0]
    tm = min(tm, s)
    nm = kv.shape[0]
    return pl.pallas_call(
        _mem_attn_kernel,
        grid=(s // tm,),
        in_specs=[pl.BlockSpec((tm, D_W), lambda i: (i, P1_DQ // D_W)),
                  pl.BlockSpec((nm, 2 * D_W), lambda i: (0, 0))],
        out_specs=pl.BlockSpec((tm, D_W), lambda i: (i, 0)),
        out_shape=jax.ShapeDtypeStruct((s, D_W), BF16),
        compiler_params=_params(("parallel",)),
        name="mem_attention",
    )(p1, kv)


def _merge_kernel(h_ref, ya_ref, yb_ref, yc_ref, yd_ref, wm0, wm1, wm2, wm3, wa, wb, wc, wd, z_ref):
    h = h_ref[...]
    z = None
    for y_ref, wm, wp in ((ya_ref, wm0, wa), (yb_ref, wm1, wb), (yc_ref, wm2, wc), (yd_ref, wm3, wd)):
        gate = jax.nn.sigmoid(jnp.dot(h, wm[...], preferred_element_type=F32))
        term = gate * jnp.dot(y_ref[...], wp[...], preferred_element_type=F32)
        z = term if z is None else z + term
    z_ref[...] = z.astype(z_ref.dtype)


def _merge(h, ya, yb, yc, yd, wm, wa, wb, wc, wd, tm=512, tn=256):
    s, d = h.shape
    tm = min(tm, s)
    nj = d // tn
    row = lambda w: pl.BlockSpec((tm, w), lambda j, i: (i, 0))
    col = lambda k, b=0: pl.BlockSpec((k, tn), lambda j, i, b=b: (0, b * nj + j))
    return pl.pallas_call(
        _merge_kernel,
        grid=(nj, s // tm),
        in_specs=[row(d), row(ya.shape[1]), row(yb.shape[1]), row(yc.shape[1]), row(yd.shape[1]),
                  col(d, 0), col(d, 1), col(d, 2), col(d, 3),
                  col(wa.shape[0]), col(wb.shape[0]), col(wc.shape[0]), col(wd.shape[0])],
        out_specs=pl.BlockSpec((tm, tn), lambda j, i: (i, j)),
        out_shape=jax.ShapeDtypeStruct((s, d), BF16),
        compiler_params=_params(("parallel", "parallel")),
        name="gated_merge",
    )(h, ya, yb, yc, yd, wm, wm, wm, wm, wa, wb, wc, wd)


def _pack_in_proj(w):
    kr = w[:, P1_CKV + A_KV_RANK:IN_AGATE]
    half = A_ROPE // 2
    kr_rot = jnp.concatenate([-kr[:, half:], kr[:, :half]], axis=1)
    zeros = lambda n: jnp.zeros((w.shape[0], n), w.dtype)
    w1 = jnp.concatenate([
        w[:, :P1_CKV + A_KV_RANK], w[:, IN_DQ:IN_MERGE],
        kr, zeros(LANES - A_ROPE), kr_rot, zeros(LANES - A_ROPE),
        w[:, IN_CIF:IN_DQ], zeros(P1_W - P1_CIF - (IN_DQ - IN_CIF))], axis=1)
    return w1.astype(BF16), w[:, IN_AGATE:IN_CIF].astype(BF16), w[:, IN_MERGE:].astype(BF16)


def _pack_uq(w_uq):
    r = w_uq.shape[0]
    w = w_uq.reshape(r, A_HEADS, A_NOPE + A_ROPE)
    nope, rope = w[..., :A_NOPE], w[..., A_NOPE:]
    half = A_ROPE // 2
    rot = jnp.concatenate([-rope[..., half:], rope[..., :half]], axis=-1)
    z = jnp.zeros((r, A_HEADS, LANES - A_ROPE), w.dtype)
    packed = jnp.concatenate([nope, rope, z, rot, z], axis=-1)
    return jnp.transpose(packed, (1, 0, 2)).astype(BF16)


def _pack_ukv(w_ukv):
    r = w_ukv.shape[0]
    w = w_ukv.reshape(r, A_HEADS, A_NOPE + A_VDIM)
    return jnp.transpose(w, (1, 0, 2)).astype(BF16)


def _layer(x, mem, pos, invf, norm_g, w_in, cq_g, w_uq, ckv_g, w_ukv, conv_w, conv_b, ln_g, ln_b,
           gate_b, mlstm_g, mem_g, w_mem_kv, w_pa, w_pb, w_pc, w_pd, w_out):
    w1, w2, wm = _pack_in_proj(w_in)
    h = _rmsnorm(x, norm_g, BF16)
    p1 = _matmul(h, w1, F32, tm=1024, tn=1280, name="in_proj_f32")
    p2 = _matmul(h, w2, BF16, tm=1024, tn=1024, name="in_proj_bf16")

    q, k, v = _mla_prep(p1, pos, invf, cq_g.reshape(1, -1), ckv_g.reshape(1, -1), _pack_uq(w_uq), _pack_ukv(w_ukv))
    y_a = _attention(q, k, v, p2)

    conv_w_pad = jnp.concatenate([conv_w, jnp.zeros((1, BRANCH_W), F32)], axis=0)
    y_b = _conv_module(p2, conv_w_pad, conv_b.reshape(1, -1), ln_g.reshape(1, -1), ln_b.reshape(1, -1))

    bias = jnp.zeros((1, LANES), F32).at[0, :4 * C_HEADS].set(gate_b.reshape(-1))
    hf, hb = _mlstm(p1, p2, bias)
    y_c = _mlstm_out(hf, hb, p2, mlstm_g.reshape(1, -1))

    kv = _mem_kv(mem, mem_g.reshape(1, -1), w_mem_kv.astype(BF16))
    y_d = _mem_attention(p1, kv)

    z = _merge(h, y_a, y_b, y_c, y_d, wm, w_pa.astype(BF16), w_pb.astype(BF16), w_pc.astype(BF16),
               w_pd.astype(BF16))
    return _matmul(z, w_out.astype(BF16), F32, tm=1024, tn=1024, residual=x, name="out_proj")


def kernel(x, mem, positions, norm_g, w_in, mla_cq_norm_g, w_uq, mla_ckv_norm_g, w_ukv, conv_w, conv_b,
           conv_ln_g, conv_ln_b, mlstm_gate_b, mlstm_norm_g, mem_norm_g, w_mem_kv, w_proj_a, w_proj_b,
           w_proj_c, w_proj_d, w_out, final_norm_g):
    b, s, d = x.shape
    depth = w_in.shape[0]
    inv_freq = ROPE_THETA ** (-jnp.arange(0, A_ROPE, 2, dtype=F32) / A_ROPE)
    invf = jnp.tile(inv_freq, LANES // (A_ROPE // 2)).reshape(1, LANES)
    outs = []
    for bi in range(b):
        xb = x[bi]
        pos = positions[bi].astype(F32).reshape(s, 1)
        for l in range(depth):
            xb = _layer(xb, mem[bi], pos, invf, norm_g[l], w_in[l], mla_cq_norm_g[l], w_uq[l],
                        mla_ckv_norm_g[l], w_ukv[l], conv_w[l], conv_b[l], conv_ln_g[l], conv_ln_b[l],
                        mlstm_gate_b[l], mlstm_norm_g[l], mem_norm_g[l], w_mem_kv[l], w_proj_a[l],
                        w_proj_b[l], w_proj_c[l], w_proj_d[l], w_out[l])
        outs.append(_rmsnorm(xb, final_norm_g, x.dtype))
    return jnp.stack(outs, axis=0)
```

```python
import functools

import jax
import jax.numpy as jnp
from jax import lax
from jax.experimental import pallas as pl
from jax.experimental.pallas import tpu as pltpu

F32 = jnp.float32
BF16 = jnp.bfloat16

D_MODEL = 4096
BRANCH_W = D_MODEL // 2
A_HEADS = 16
A_NOPE = 128
A_ROPE = 64
A_VDIM = 128
A_Q_RANK = 1024
A_KV_RANK = 512
ROPE_THETA = 10000.0
CONV_W = 31
C_HEADS = 4
C_DK = 256
C_DV = 512
D_HEADS = 4
D_HDIM = 128
D_W = D_HEADS * D_HDIM
N_BRANCH = 4
EPS = 1e-6
LOG2_E = 1.4426950408889634

LANES = 128
A_QK = 2 * LANES
A_VEXT = 2 * LANES
ATTN_ROWS = 256
MLSTM_CHUNK = 256
CONV_HALO = 16
VMEM_LIMIT = 56 * 1024 * 1024

P1_CQ, P1_CKV, P1_DQ, P1_KR, P1_KRROT, P1_CIF, P1_W = 0, 1024, 1536, 2048, 2176, 2304, 2560
P2_AGATE, P2_GLU_A, P2_GLU_G, P2_BGATE = 0, 2048, 4096, 6144
P2_CQ, P2_CK, P2_CV, P2_CO, P2_CGATE, P2_W = 8192, 9216, 10240, 12288, 14336, 16384
IN_AGATE, IN_CIF, IN_DQ, IN_MERGE = 1600, 17984, 18000, 18512


def _params(semantics):
    return pltpu.CompilerParams(dimension_semantics=semantics, vmem_limit_bytes=VMEM_LIMIT)


def _silu(x):
    return x * jax.nn.sigmoid(x)


def _rms(x, g):
    return x * lax.rsqrt(jnp.mean(x * x, axis=-1, keepdims=True) + EPS) * g


def _rmsnorm_kernel(x_ref, g_ref, o_ref):
    o_ref[...] = _rms(x_ref[...], g_ref[...]).astype(o_ref.dtype)


def _rmsnorm(x, g, out_dtype, tm=512):
    m, d = x.shape
    tm = min(tm, m)
    return pl.pallas_call(
        _rmsnorm_kernel,
        grid=(m // tm,),
        in_specs=[pl.BlockSpec((tm, d), lambda i: (i, 0)), pl.BlockSpec((1, d), lambda i: (0, 0))],
        out_specs=pl.BlockSpec((tm, d), lambda i: (i, 0)),
        out_shape=jax.ShapeDtypeStruct((m, d), out_dtype),
        compiler_params=_params(("parallel",)),
        name="rmsnorm",
    )(x, g.reshape(1, d))


def _mm_kernel(a_ref, w_ref, o_ref):
    o_ref[...] = jnp.dot(a_ref[...], w_ref[...], preferred_element_type=F32).astype(o_ref.dtype)


def _mm_res_kernel(a_ref, w_ref, r_ref, o_ref):
    acc = jnp.dot(a_ref[...], w_ref[...], preferred_element_type=F32)
    o_ref[...] = (r_ref[...] + acc).astype(o_ref.dtype)


def _matmul(a, w, out_dtype, *, tm, tn, residual=None, name="matmul"):
    m, k = a.shape
    n = w.shape[1]
    tm, tn = min(tm, m), min(tn, n)
    in_specs = [pl.BlockSpec((tm, k), lambda j, i: (i, 0)), pl.BlockSpec((k, tn), lambda j, i: (0, j))]
    args = [a, w]
    kern = _mm_kernel
    if residual is not None:
        in_specs.append(pl.BlockSpec((tm, tn), lambda j, i: (i, j)))
        args.append(residual)
        kern = _mm_res_kernel
    return pl.pallas_call(
        kern,
        grid=(n // tn, m // tm),
        in_specs=in_specs,
        out_specs=pl.BlockSpec((tm, tn), lambda j, i: (i, j)),
        out_shape=jax.ShapeDtypeStruct((m, n), out_dtype),
        compiler_params=_params(("parallel", "parallel")),
        name=name,
    )(*args)


REPACK_W = 512


def _repack_kernel(off, *refs):
    *src, o_ref = refs
    x = jnp.concatenate([r[0] for r in src], axis=1)
    o_ref[...] = x[:, off:off + o_ref.shape[1]].astype(o_ref.dtype)


def _repack_cols(w, layer, start, ncols, tk=1024, tn=1024):
    k = w.shape[1]
    base, off = divmod(start, REPACK_W)
    nb = tn // REPACK_W
    src = lambda d: pl.BlockSpec((1, tk, REPACK_W), lambda j, kk, d=d: (layer, kk, base + nb * j + d))
    return pl.pallas_call(
        functools.partial(_repack_kernel, off),
        grid=(ncols // tn, k // tk),
        in_specs=[src(d) for d in range(nb + 1)],
        out_specs=pl.BlockSpec((tk, tn), lambda j, kk: (kk, j)),
        out_shape=jax.ShapeDtypeStruct((k, ncols), BF16),
        compiler_params=_params(("parallel", "parallel")),
        name="repack_cols",
    )(*([w] * (nb + 1)))


def _mla_prep_kernel(cq_ref, ckv_ref, kr_ref, krrot_ref, pos_ref, invf_ref, gq_ref, gkv_ref,
                     wq_ref, wkv_ref, q_ref, k_ref, v_ref, cnq_s, cnkv_s, krope_s, cos_s, sin_s):
    @pl.when(pl.program_id(1) == 0)
    def _():
        ang = pos_ref[...] * invf_ref[...]
        cos, sin = jnp.cos(ang), jnp.sin(ang)
        cos_s[...] = cos
        sin_s[...] = sin
        cnq_s[...] = _rms(cq_ref[...], gq_ref[...]).astype(BF16)
        cnkv_s[...] = _rms(ckv_ref[...], gkv_ref[...]).astype(BF16)
        krope_s[...] = (kr_ref[...] * cos + krrot_ref[...] * sin).astype(BF16)

    qh = jnp.dot(cnq_s[...], wq_ref[0], preferred_element_type=F32)
    q_rope = qh[:, LANES:2 * LANES] * cos_s[...] + qh[:, 2 * LANES:] * sin_s[...]
    q_ref[0, :, :LANES] = qh[:, :LANES].astype(BF16)
    q_ref[0, :, LANES:] = q_rope.astype(BF16)
    kvh = jnp.dot(cnkv_s[...], wkv_ref[0], preferred_element_type=F32)
    k_ref[0, :, :LANES] = kvh[:, :LANES].astype(BF16)
    k_ref[0, :, LANES:] = krope_s[...]
    v_ref[0, :, :LANES] = kvh[:, LANES:].astype(BF16)
    lane = lax.broadcasted_iota(jnp.int32, (v_ref.shape[1], LANES), 1)
    v_ref[0, :, LANES:] = (lane == 0).astype(BF16)


def _mla_prep(p1, pos, invf, gq, gkv, wq, wkv, tm=512):
    s = p1.shape[0]
    tm = min(tm, s)
    row = lambda w, c: pl.BlockSpec((tm, w), lambda i, h, c=c: (i, c))
    const = lambda shape: pl.BlockSpec(shape, lambda i, h: (0,) * len(shape))
    head_out = lambda w: pl.BlockSpec((1, tm, w), lambda i, h: (h, i, 0))
    return pl.pallas_call(
        _mla_prep_kernel,
        grid=(s // tm, A_HEADS),
        in_specs=[
            row(A_Q_RANK, P1_CQ // A_Q_RANK), row(A_KV_RANK, P1_CKV // A_KV_RANK),
            row(LANES, P1_KR // LANES), row(LANES, P1_KRROT // LANES),
            pl.BlockSpec((tm, 1), lambda i, h: (i, 0)), const((1, LANES)),
            const((1, A_Q_RANK)), const((1, A_KV_RANK)),
            pl.BlockSpec((1, A_Q_RANK, 3 * LANES), lambda i, h: (h, 0, 0)),
            pl.BlockSpec((1, A_KV_RANK, 2 * LANES), lambda i, h: (h, 0, 0)),
        ],
        out_specs=[head_out(A_QK), head_out(A_QK), head_out(A_VEXT)],
        out_shape=[jax.ShapeDtypeStruct((A_HEADS, s, A_QK), BF16),
                   jax.ShapeDtypeStruct((A_HEADS, s, A_QK), BF16),
                   jax.ShapeDtypeStruct((A_HEADS, s, A_VEXT), BF16)],
        scratch_shapes=[pltpu.VMEM((tm, A_Q_RANK), BF16), pltpu.VMEM((tm, A_KV_RANK), BF16),
                        pltpu.VMEM((tm, LANES), BF16), pltpu.VMEM((tm, LANES), F32),
                        pltpu.VMEM((tm, LANES), F32)],
        compiler_params=_params(("parallel", "arbitrary")),
        name="mla_prep",
    )(p1, p1, p1, p1, pos, invf, gq, gkv, wq, wkv)


def _attn_kernel(q_ref, k_ref, v_ref, gate_ref, o_ref, m_s, alpha_s, acc_s, sa_s, sb_s):
    kv = pl.program_id(2)
    tq = acc_s.shape[0]
    rc, tk = sa_s.shape
    nrc = tq // rc

    @pl.when(kv == 0)
    def _():
        m_s[...] = jnp.full(m_s.shape, -jnp.inf, F32)
        acc_s[...] = jnp.zeros(acc_s.shape, F32)

    def scores(c, s_ref):
        rows = pl.ds(c * rc, rc)
        s = lax.dot_general(q_ref[0, rows, :], k_ref[0], (((1,), (1,)), ((), ())), preferred_element_type=F32)
        s_ref[...] = s
        m_prev = m_s[rows, :]
        m_new = jnp.maximum(m_prev, jnp.max(s, axis=-1, keepdims=True))
        alpha_s[rows, :] = jnp.exp2(m_prev - m_new)
        m_s[rows, :] = m_new

    def accumulate(c, s_ref):
        rows = pl.ds(c * rc, rc)
        p = jnp.exp2(s_ref[...] - pltpu.repeat(m_s[rows, :], tk // LANES, axis=1))
        acc_s[rows, :] = pltpu.repeat(alpha_s[rows, :], A_VEXT // LANES, axis=1) * acc_s[rows, :] + jnp.dot(
            p.astype(BF16), v_ref[0], preferred_element_type=F32)

    bufs = (sa_s, sb_s)
    scores(0, bufs[0])
    for c in range(nrc):
        accumulate(c, bufs[c % 2])
        if c + 1 < nrc:
            scores(c + 1, bufs[(c + 1) % 2])

    @pl.when(kv == pl.num_programs(2) - 1)
    def _():
        acc = acc_s[...]
        l = jnp.max(acc[:, A_VDIM:], axis=-1, keepdims=True)
        o_ref[...] = (acc[:, :A_VDIM] / l * _silu(gate_ref[...].astype(F32))).astype(o_ref.dtype)


def _attention(q, k, v, p2, tq=4096, tk=2048):
    h, s, _ = q.shape
    tq, tk = min(tq, s), min(tk, s)
    rc = min(ATTN_ROWS, tq)
    return pl.pallas_call(
        _attn_kernel,
        grid=(h, s // tq, s // tk),
        in_specs=[
            pl.BlockSpec((1, tq, A_QK), lambda h, i, j: (h, i, 0)),
            pl.BlockSpec((1, tk, A_QK), lambda h, i, j: (h, j, 0)),
            pl.BlockSpec((1, tk, A_VEXT), lambda h, i, j: (h, j, 0)),
            pl.BlockSpec((tq, A_VDIM), lambda h, i, j: (i, P2_AGATE // A_VDIM + h)),
        ],
        out_specs=pl.BlockSpec((tq, A_VDIM), lambda h, i, j: (i, h)),
        out_shape=jax.ShapeDtypeStruct((s, A_HEADS * A_VDIM), BF16),
        scratch_shapes=[pltpu.VMEM((tq, LANES), F32), pltpu.VMEM((tq, LANES), F32), pltpu.VMEM((tq, A_VEXT), F32),
                        pltpu.VMEM((rc, tk), F32), pltpu.VMEM((rc, tk), F32)],
        compiler_params=_params(("parallel", "parallel", "arbitrary")),
        name="mla_attention",
    )(q, k, v, p2)


CONV_CW = 128
CONV_ROWS = 128
SUBLANES = 8
CONV_OFF = CONV_HALO - CONV_W // 2
CONV_REACH = (CONV_OFF + CONV_W - 1) // SUBLANES * SUBLANES


def _conv_kernel(a_ref, g_ref, ap_ref, gp_ref, an_ref, gn_ref, bg_ref, w_ref, cb_ref, lg_ref, lb_ref,
                 o_ref, u_s, c_s, sh_s):
    i = pl.program_id(0)
    tm = a_ref.shape[0]
    nchunk = BRANCH_W // CONV_CW
    glu = lambda a, g: a.astype(F32) * jax.nn.sigmoid(g.astype(F32))
    first, last = i > 0, i < pl.num_programs(0) - 1
    for ci in range(nchunk):
        cs = slice(ci * CONV_CW, (ci + 1) * CONV_CW)
        u_s[ci, CONV_HALO:CONV_HALO + tm, :] = glu(a_ref[:, cs], g_ref[:, cs])
        u_s[ci, :CONV_HALO, :] = jnp.where(first, glu(ap_ref[:, cs], gp_ref[:, cs]), 0.0)
        u_s[ci, CONV_HALO + tm:, :] = jnp.where(last, glu(an_ref[:, cs], gn_ref[:, cs]), 0.0)

    off, reach = CONV_OFF, CONV_REACH
    rows = min(CONV_ROWS, tm)

    def chunk(ci, carry):
        for r0 in range(0, tm, rows):
            acc = jnp.broadcast_to(cb_ref[ci], (rows, CONV_CW))
            for b in range(SUBLANES):
                sh_s[b] = u_s[ci, r0 + b:r0 + b + rows + reach, :]
                for t in range(CONV_W):
                    if (off + t) % SUBLANES == b:
                        a = off + t - b
                        acc = acc + sh_s[b, a:a + rows, :] * w_ref[ci, t:t + 1, :]
            c_s[ci, r0:r0 + rows, :] = acc
        return carry

    lax.fori_loop(0, nchunk, chunk, 0)

    total = c_s[0]
    for ci in range(1, nchunk):
        total = total + c_s[ci]
    mean = jnp.sum(total, axis=-1, keepdims=True) * (1.0 / BRANCH_W)
    sq = jnp.zeros_like(total)
    for ci in range(nchunk):
        d = c_s[ci] - mean
        sq = sq + d * d
    rstd = lax.rsqrt(jnp.sum(sq, axis=-1, keepdims=True) * (1.0 / BRANCH_W) + EPS)
    for ci in range(nchunk):
        cs = slice(ci * CONV_CW, (ci + 1) * CONV_CW)
        y = (c_s[ci] - mean) * rstd * lg_ref[:, cs] + lb_ref[:, cs]
        o_ref[:, cs] = (_silu(y) * _silu(bg_ref[:, cs].astype(F32))).astype(o_ref.dtype)


def _conv_module(p2, w, cb, lg, lb, tm=256):
    s = p2.shape[0]
    tm = min(tm, s)
    r = tm // CONV_HALO
    nh = s // CONV_HALO
    wide = lambda c: pl.BlockSpec((tm, BRANCH_W), lambda i, c=c: (i, c))
    prev = lambda c: pl.BlockSpec((CONV_HALO, BRANCH_W), lambda i, c=c: (jnp.maximum(i * r - 1, 0), c))
    nxt = lambda c: pl.BlockSpec((CONV_HALO, BRANCH_W), lambda i, c=c: (jnp.minimum((i + 1) * r, nh - 1), c))
    const = lambda rows: pl.BlockSpec((rows, BRANCH_W), lambda i: (0, 0))
    nchunk = BRANCH_W // CONV_CW
    chunked = lambda rows: pl.BlockSpec((nchunk, rows, CONV_CW), lambda i: (0, 0, 0))
    ca, cg, cbg = P2_GLU_A // BRANCH_W, P2_GLU_G // BRANCH_W, P2_BGATE // BRANCH_W
    w_c = jnp.transpose(w.reshape(w.shape[0], nchunk, CONV_CW), (1, 0, 2))
    cb_c = cb.reshape(nchunk, 1, CONV_CW)
    return pl.pallas_call(
        _conv_kernel,
        grid=(s // tm,),
        in_specs=[wide(ca), wide(cg), prev(ca), prev(cg), nxt(ca), nxt(cg), wide(cbg),
                  chunked(w.shape[0]), chunked(1), const(1), const(1)],
        out_specs=pl.BlockSpec((tm, BRANCH_W), lambda i: (i, 0)),
        out_shape=jax.ShapeDtypeStruct((s, BRANCH_W), BF16),
        scratch_shapes=[pltpu.VMEM((nchunk, tm + 2 * CONV_HALO, CONV_CW), F32),
                        pltpu.VMEM((nchunk, tm, CONV_CW), F32),
                        pltpu.VMEM((SUBLANES, min(CONV_ROWS, tm) + CONV_REACH, CONV_CW), F32)],
        compiler_params=_params(("parallel",)),
        name="conv_module",
    )(p2, p2, p2, p2, p2, p2, p2, w_c, cb_c, lg, lb)


def _mlstm_kernel(qf_ref, kf_ref, vf_ref, gf_ref, qb_ref, kb_ref, vb_ref, gb_ref, bias_ref,
                  hf_ref, hb_ref, c_s, n_s, m_s):
    L = qf_ref.shape[0]

    @pl.when(pl.program_id(0) == 0)
    def _():
        c_s[...] = jnp.zeros(c_s.shape, F32)
        n_s[...] = jnp.zeros(n_s.shape, F32)
        m_s[...] = jnp.zeros(m_s.shape, F32)

    rows = lax.broadcasted_iota(jnp.int32, (L, L), 0)
    cols = lax.broadcasted_iota(jnp.int32, (L, L), 1)
    lower = rows >= cols
    upper = rows <= cols
    tri_l = lower.astype(F32)
    tri_u = upper.astype(F32)
    hi = lax.Precision.HIGHEST

    for d, (q_ref, k_ref, v_ref, g_ref, h_ref) in enumerate(
            ((qf_ref, kf_ref, vf_ref, gf_ref, hf_ref), (qb_ref, kb_ref, vb_ref, gb_ref, hb_ref))):
        g = g_ref[...] + bias_ref[...]
        g_t = g.T
        lf = jax.nn.log_sigmoid(g)
        lf_t = jax.nn.log_sigmoid(g_t)
        mask = lower if d == 0 else upper
        bcum = jnp.dot(tri_l if d == 0 else tri_u, lf, precision=hi, preferred_element_type=F32)
        bcum_t = jnp.dot(lf_t, tri_u if d == 0 else tri_l, precision=hi, preferred_element_type=F32)
        last = L - 1 if d == 0 else 0
        for h in range(C_HEADS):
            ci, cf, sidx = d * 8 + h, d * 8 + 4 + h, d * C_HEADS + h
            qh = q_ref[:, h * C_DK:(h + 1) * C_DK]
            kh = k_ref[:, h * C_DK:(h + 1) * C_DK] * (C_DK ** -0.5)
            vh = v_ref[:, h * C_DV:(h + 1) * C_DV]
            bcol, brow = bcum[:, cf:cf + 1], bcum_t[cf:cf + 1, :]
            icol, irow = g[:, ci:ci + 1], g_t[ci:ci + 1, :]
            m_prev = m_s[sidx]
            c_prev = c_s[sidx]
            n_prev = n_s[sidx]

            logw = jnp.where(mask, bcol - brow + irow, -jnp.inf)
            inter = bcol + m_prev
            m_loc = jnp.maximum(inter, jnp.max(logw, axis=-1, keepdims=True))
            w_intra = jnp.exp(logw - m_loc)
            w_inter = jnp.exp(inter - m_loc)
            s = lax.dot_general(qh, kh, (((1,), (1,)), ((), ())), preferred_element_type=F32) * w_intra
            num = (jnp.dot(s.astype(BF16), vh, preferred_element_type=F32)
                   + w_inter * jnp.dot(qh, c_prev.astype(BF16), preferred_element_type=F32))
            den = (jnp.sum(s, axis=-1, keepdims=True)
                   + w_inter * jnp.sum(qh.astype(F32) * n_prev, axis=-1, keepdims=True))
            h_ref[:, h * C_DV:(h + 1) * C_DV] = num / jnp.maximum(jnp.abs(den), jnp.exp(-m_loc))

            b_tot = bcol[last:last + 1, :]
            lw_end = b_tot - bcol + icol
            m_new = jnp.maximum(b_tot + m_prev, jnp.max(lw_end, axis=0, keepdims=True))
            decay = jnp.exp(b_tot + m_prev - m_new)
            wk = jnp.exp(lw_end - m_new) * kh.astype(F32)
            c_s[sidx] = decay * c_prev + lax.dot_general(
                wk.astype(BF16), vh, (((0,), (0,)), ((), ())), preferred_element_type=F32)
            n_s[sidx] = decay * n_prev + jnp.sum(wk, axis=0, keepdims=True)
            m_s[sidx] = m_new


def _mlstm(p1, p2, bias):
    s = p2.shape[0]
    L = min(MLSTM_CHUNK, s)
    nc = s // L
    wq, wv = C_HEADS * C_DK, C_HEADS * C_DV
    fwd = lambda w, c: pl.BlockSpec((L, w), lambda i, c=c: (i, c))
    bwd = lambda w, c: pl.BlockSpec((L, w), lambda i, c=c: (nc - 1 - i, c))
    specs = lambda mk: [mk(wq, P2_CQ // wq), mk(wq, P2_CK // wq), mk(wv, P2_CV // wv)]
    return pl.pallas_call(
        _mlstm_kernel,
        grid=(nc,),
        in_specs=specs(fwd) + [fwd(LANES, P1_CIF // LANES)] + specs(bwd) + [bwd(LANES, P1_CIF // LANES)]
        + [pl.BlockSpec((1, LANES), lambda i: (0, 0))],
        out_specs=[pl.BlockSpec((L, wv), lambda i: (i, 0)), pl.BlockSpec((L, wv), lambda i: (nc - 1 - i, 0))],
        out_shape=[jax.ShapeDtypeStruct((s, wv), F32), jax.ShapeDtypeStruct((s, wv), F32)],
        scratch_shapes=[pltpu.VMEM((2 * C_HEADS, C_DK, C_DV), F32), pltpu.VMEM((2 * C_HEADS, 1, C_DK), F32),
                        pltpu.VMEM((2 * C_HEADS, 1, 1), F32)],
        compiler_params=_params(("arbitrary",)),
        name="mlstm_scan",
    )(p2, p2, p2, p1, p2, p2, p2, p1, bias)


def _mlstm_out_kernel(hf_ref, hb_ref, o_ref, gate_ref, ng_ref, y_ref):
    for h in range(C_HEADS):
        sl = slice(h * C_DV, (h + 1) * C_DV)
        x = hf_ref[:, sl] + hb_ref[:, sl]
        x = x * lax.rsqrt(jnp.mean(x * x, axis=-1, keepdims=True) + EPS) * ng_ref[:, sl]
        x = jax.nn.sigmoid(o_ref[:, sl].astype(F32)) * x
        y_ref[:, sl] = (x * _silu(gate_ref[:, sl].astype(F32))).astype(y_ref.dtype)


def _mlstm_out(hf, hb, p2, ng, tm=256):
    s, w = hf.shape
    tm = min(tm, s)
    row = lambda c: pl.BlockSpec((tm, w), lambda i, c=c: (i, c))
    return pl.pallas_call(
        _mlstm_out_kernel,
        grid=(s // tm,),
        in_specs=[row(0), row(0), row(P2_CO // w), row(P2_CGATE // w), pl.BlockSpec((1, w), lambda i: (0, 0))],
        out_specs=row(0),
        out_shape=jax.ShapeDtypeStruct((s, w), BF16),
        compiler_params=_params(("parallel",)),
        name="mlstm_out",
    )(hf, hb, p2, p2, ng)


def _mem_kv_kernel(mem_ref, g_ref, w_ref, o_ref):
    hn = _rms(mem_ref[...], g_ref[...]).astype(BF16)
    o_ref[...] = jnp.dot(hn, w_ref[...], preferred_element_type=F32).astype(o_ref.dtype)


def _mem_kv(mem, g, w):
    nm, d = mem.shape
    n = w.shape[1]
    tn = 512
    return pl.pallas_call(
        _mem_kv_kernel,
        grid=(n // tn,),
        in_specs=[pl.BlockSpec((nm, d), lambda j: (0, 0)), pl.BlockSpec((1, d), lambda j: (0, 0)),
                  pl.BlockSpec((d, tn), lambda j: (0, j))],
        out_specs=pl.BlockSpec((nm, tn), lambda j: (0, j)),
        out_shape=jax.ShapeDtypeStruct((nm, n), BF16),
        compiler_params=_params(("parallel",)),
        name="mem_kv",
    )(mem, g, w)


def _mem_attn_kernel(q_ref, kv_ref, o_ref):
    scale = D_HDIM ** -0.5
    for h in range(D_HEADS):
        sl = slice(h * D_HDIM, (h + 1) * D_HDIM)
        q = (q_ref[:, sl] * scale).astype(BF16)
        s = lax.dot_general(q, kv_ref[:, sl], (((1,), (1,)), ((), ())), preferred_element_type=F32)
        p = jnp.exp(s - jnp.max(s, axis=-1, keepdims=True))
        l = jnp.sum(p, axis=-1, keepdims=True)
        v = kv_ref[:, D_W + h * D_HDIM:D_W + (h + 1) * D_HDIM]
        o_ref[:, sl] = (jnp.dot(p.astype(BF16), v, preferred_element_type=F32) / l).astype(o_ref.dtype)


def _mem_attention(p1, kv, tm=512):
    s = p1.shape[0]
    tm = min(tm, s)
    nm = kv.shape[0]
    return pl.pallas_call(
        _mem_attn_kernel,
        grid=(s // tm,),
        in_specs=[pl.BlockSpec((tm, D_W), lambda i: (i, P1_DQ // D_W)),
                  pl.BlockSpec((nm, 2 * D_W), lambda i: (0, 0))],
        out_specs=pl.BlockSpec((tm, D_W), lambda i: (i, 0)),
        out_shape=jax.ShapeDtypeStruct((s, D_W), BF16),
        compiler_params=_params(("parallel",)),
        name="mem_attention",
    )(p1, kv)


def _merge_kernel(h_ref, ya_ref, yb_ref, yc_ref, yd_ref, wm0, wm1, wm2, wm3, wa, wb, wc, wd, z_ref):
    h = h_ref[...]
    z = None
    for y_ref, wm, wp in ((ya_ref, wm0, wa), (yb_ref, wm1, wb), (yc_ref, wm2, wc), (yd_ref, wm3, wd)):
        gate = jax.nn.sigmoid(jnp.dot(h, wm[...], preferred_element_type=F32))
        term = gate * jnp.dot(y_ref[...], wp[...], preferred_element_type=F32)
        z = term if z is None else z + term
    z_ref[...] = z.astype(z_ref.dtype)


def _merge(h, ya, yb, yc, yd, wm, wa, wb, wc, wd, tm=512, tn=256):
    s, d = h.shape
    tm = min(tm, s)
    nj = d // tn
    row = lambda w: pl.BlockSpec((tm, w), lambda j, i: (i, 0))
    col = lambda k, b=0: pl.BlockSpec((k, tn), lambda j, i, b=b: (0, b * nj + j))
    return pl.pallas_call(
        _merge_kernel,
        grid=(nj, s // tm),
        in_specs=[row(d), row(ya.shape[1]), row(yb.shape[1]), row(yc.shape[1]), row(yd.shape[1]),
                  col(d, 0), col(d, 1), col(d, 2), col(d, 3),
                  col(wa.shape[0]), col(wb.shape[0]), col(wc.shape[0]), col(wd.shape[0])],
        out_specs=pl.BlockSpec((tm, tn), lambda j, i: (i, j)),
        out_shape=jax.ShapeDtypeStruct((s, d), BF16),
        compiler_params=_params(("parallel", "parallel")),
        name="gated_merge",
    )(h, ya, yb, yc, yd, wm, wm, wm, wm, wa, wb, wc, wd)


def _pack_in_proj(w_in, layer):
    cols = lambda a, b: lax.slice(w_in, (layer, 0, a), (layer + 1, w_in.shape[1], b))[0]
    kr = cols(P1_CKV + A_KV_RANK, IN_AGATE)
    half = A_ROPE // 2
    kr_rot = jnp.concatenate([-kr[:, half:], kr[:, :half]], axis=1)
    zeros = lambda n: jnp.zeros((w_in.shape[1], n), w_in.dtype)
    w1 = jnp.concatenate([
        cols(0, P1_CKV + A_KV_RANK), cols(IN_DQ, IN_MERGE),
        kr, zeros(LANES - A_ROPE), kr_rot, zeros(LANES - A_ROPE),
        cols(IN_CIF, IN_DQ), zeros(P1_W - P1_CIF - (IN_DQ - IN_CIF))], axis=1)
    w2 = _repack_cols(w_in, layer, IN_AGATE, IN_CIF - IN_AGATE)
    wm = _repack_cols(w_in, layer, IN_MERGE, w_in.shape[2] - IN_MERGE)
    return w1.astype(BF16), w2, wm


def _pack_uq(w_uq):
    r = w_uq.shape[0]
    w = w_uq.reshape(r, A_HEADS, A_NOPE + A_ROPE)
    nope, rope = w[..., :A_NOPE], w[..., A_NOPE:]
    half = A_ROPE // 2
    rot = jnp.concatenate([-rope[..., half:], rope[..., :half]], axis=-1)
    z = jnp.zeros((r, A_HEADS, LANES - A_ROPE), w.dtype)
    packed = jnp.concatenate([nope, rope, z, rot, z], axis=-1)
    return jnp.transpose(packed, (1, 0, 2)).astype(BF16)


def _pack_ukv(w_ukv):
    r = w_ukv.shape[0]
    w = w_ukv.reshape(r, A_HEADS, A_NOPE + A_VDIM)
    return jnp.transpose(w, (1, 0, 2)).astype(BF16)


def _layer(x, mem, pos, invf, norm_g, w_in, layer, cq_g, w_uq, ckv_g, w_ukv, conv_w, conv_b, ln_g, ln_b,
           gate_b, mlstm_g, mem_g, w_mem_kv, w_pa, w_pb, w_pc, w_pd, w_out):
    w1, w2, wm = _pack_in_proj(w_in, layer)
    h = _rmsnorm(x, norm_g, BF16)
    p1 = _matmul(h, w1, F32, tm=1024, tn=1280, name="in_proj_f32")
    p2 = _matmul(h, w2, BF16, tm=1024, tn=1024, name="in_proj_bf16")

    q_gain = cq_g * ((A_NOPE + A_ROPE) ** -0.5 * LOG2_E)
    q, k, v = _mla_prep(p1, pos, invf, q_gain.reshape(1, -1), ckv_g.reshape(1, -1), _pack_uq(w_uq), _pack_ukv(w_ukv))
    y_a = _attention(q, k, v, p2)

    conv_w_pad = jnp.concatenate([conv_w, jnp.zeros((1, BRANCH_W), F32)], axis=0)
    y_b = _conv_module(p2, conv_w_pad, conv_b.reshape(1, -1), ln_g.reshape(1, -1), ln_b.reshape(1, -1))

    bias = jnp.zeros((1, LANES), F32).at[0, :4 * C_HEADS].set(gate_b.reshape(-1))
    hf, hb = _mlstm(p1, p2, bias)
    y_c = _mlstm_out(hf, hb, p2, mlstm_g.reshape(1, -1))

    kv = _mem_kv(mem, mem_g.reshape(1, -1), w_mem_kv.astype(BF16))
    y_d = _mem_attention(p1, kv)

    z = _merge(h, y_a, y_b, y_c, y_d, wm, w_pa.astype(BF16), w_pb.astype(BF16), w_pc.astype(BF16),
               w_pd.astype(BF16))
    return _matmul(z, w_out.astype(BF16), F32, tm=1024, tn=1024, residual=x, name="out_proj")


def kernel(x, mem, positions, norm_g, w_in, mla_cq_norm_g, w_uq, mla_ckv_norm_g, w_ukv, conv_w, conv_b,
           conv_ln_g, conv_ln_b, mlstm_gate_b, mlstm_norm_g, mem_norm_g, w_mem_kv, w_proj_a, w_proj_b,
           w_proj_c, w_proj_d, w_out, final_norm_g):
    b, s, d = x.shape
    depth = w_in.shape[0]
    inv_freq = ROPE_THETA ** (-jnp.arange(0, A_ROPE, 2, dtype=F32) / A_ROPE)
    invf = jnp.tile(inv_freq, LANES // (A_ROPE // 2)).reshape(1, LANES)
    outs = []
    for bi in range(b):
        xb = x[bi]
        pos = positions[bi].astype(F32).reshape(s, 1)
        for l in range(depth):
            xb = _layer(xb, mem[bi], pos, invf, norm_g[l], w_in, l, mla_cq_norm_g[l], w_uq[l],
                        mla_ckv_norm_g[l], w_ukv[l], conv_w[l], conv_b[l], conv_ln_g[l], conv_ln_b[l],
                        mlstm_gate_b[l], mlstm_norm_g[l], mem_norm_g[l], w_mem_kv[l], w_proj_a[l],
                        w_proj_b[l], w_proj_c[l], w_proj_d[l], w_out[l])
        outs.append(_rmsnorm(xb, final_norm_g, x.dtype))
    return jnp.stack(outs, axis=0)
```

```python
import functools

import jax
import jax.numpy as jnp
from jax import lax
from jax.experimental import pallas as pl
from jax.experimental.pallas import tpu as pltpu

F32 = jnp.float32
BF16 = jnp.bfloat16

D_MODEL = 4096
BRANCH_W = D_MODEL // 2
A_HEADS = 16
A_NOPE = 128
A_ROPE = 64
A_VDIM = 128
A_Q_RANK = 1024
A_KV_RANK = 512
ROPE_THETA = 10000.0
CONV_W = 31
C_HEADS = 4
C_DK = 256
C_DV = 512
D_HEADS = 4
D_HDIM = 128
D_W = D_HEADS * D_HDIM
N_BRANCH = 4
EPS = 1e-6
LOG2_E = 1.4426950408889634

LANES = 128
SUBLANES = 8
A_QK = 2 * LANES
A_VEXT = 2 * LANES
ATTN_ROWS = 256
MLSTM_CHUNK = 256
CONV_HALO = 16
VMEM_LIMIT = 56 * 1024 * 1024

P1_CQ, P1_CKV, P1_DQ, P1_KR, P1_KRROT, P1_CIF, P1_W = 0, 1024, 1536, 2048, 2176, 2304, 2560
P2_AGATE, P2_GLU_A, P2_GLU_G, P2_BGATE = 0, 2048, 4096, 6144
P2_CQ, P2_CK, P2_CV, P2_CO, P2_CGATE, P2_W = 8192, 9216, 10240, 12288, 14336, 16384
IN_AGATE, IN_CIF, IN_DQ, IN_MERGE = 1600, 17984, 18000, 18512


def _params(semantics):
    return pltpu.CompilerParams(dimension_semantics=semantics, vmem_limit_bytes=VMEM_LIMIT)


def _silu(x):
    return x * jax.nn.sigmoid(x)


def _lane_tile(x, n):
    return jnp.concatenate([x] * n, axis=1)


def _rms(x, g):
    return x * lax.rsqrt(jnp.mean(x * x, axis=-1, keepdims=True) + EPS) * g


def _rmsnorm_kernel(x_ref, g_ref, o_ref):
    o_ref[...] = _rms(x_ref[...], g_ref[...]).astype(o_ref.dtype)


def _rmsnorm(x, g, out_dtype, tm=512):
    m, d = x.shape
    tm = min(tm, m)
    return pl.pallas_call(
        _rmsnorm_kernel,
        grid=(m // tm,),
        in_specs=[pl.BlockSpec((tm, d), lambda i: (i, 0)), pl.BlockSpec((1, d), lambda i: (0, 0))],
        out_specs=pl.BlockSpec((tm, d), lambda i: (i, 0)),
        out_shape=jax.ShapeDtypeStruct((m, d), out_dtype),
        compiler_params=_params(("parallel",)),
        name="rmsnorm",
    )(x, g.reshape(1, d))


NT_DIMS = (((1,), (1,)), ((), ()))


def _mm_kernel(a_ref, w_ref, o_ref):
    o_ref[...] = jnp.dot(a_ref[...], w_ref[...], preferred_element_type=F32).astype(o_ref.dtype)


def _mm_nt_kernel(a_ref, w_ref, o_ref):
    o_ref[...] = lax.dot_general(a_ref[...], w_ref[...], NT_DIMS, preferred_element_type=F32).astype(o_ref.dtype)


def _mm_res_kernel(a_ref, w_ref, r_ref, o_ref):
    acc = jnp.dot(a_ref[...], w_ref[...], preferred_element_type=F32)
    o_ref[...] = (r_ref[...] + acc).astype(o_ref.dtype)


def _matmul(a, w, out_dtype, *, tm, tn, residual=None, transposed_w=False, name="matmul"):
    m, k = a.shape
    n = w.shape[0] if transposed_w else w.shape[1]
    tm, tn = min(tm, m), min(tn, n)
    w_spec = (pl.BlockSpec((tn, k), lambda j, i: (j, 0)) if transposed_w
              else pl.BlockSpec((k, tn), lambda j, i: (0, j)))
    in_specs = [pl.BlockSpec((tm, k), lambda j, i: (i, 0)), w_spec]
    args = [a, w]
    kern = _mm_nt_kernel if transposed_w else _mm_kernel
    if residual is not None:
        assert not transposed_w
        in_specs.append(pl.BlockSpec((tm, tn), lambda j, i: (i, j)))
        args.append(residual)
        kern = _mm_res_kernel
    return pl.pallas_call(
        kern,
        grid=(n // tn, m // tm),
        in_specs=in_specs,
        out_specs=pl.BlockSpec((tm, tn), lambda j, i: (i, j)),
        out_shape=jax.ShapeDtypeStruct((m, n), out_dtype),
        compiler_params=_params(("parallel", "parallel")),
        name=name,
    )(*args)


def _cast_rows_kernel(w_ref, o_ref):
    o_ref[...] = w_ref[...].astype(o_ref.dtype)


def _cast_rows(w, start, nrows, tr=512):
    k = w.shape[1]
    tr = min(tr, nrows)
    assert start % SUBLANES == 0 and nrows % tr == 0 and tr % SUBLANES == 0
    return pl.pallas_call(
        _cast_rows_kernel,
        grid=(nrows // tr,),
        in_specs=[pl.BlockSpec((pl.Element(tr), pl.Element(k)),
                               lambda i: (pl.multiple_of(start + i * tr, SUBLANES), 0))],
        out_specs=pl.BlockSpec((tr, k), lambda i: (i, 0)),
        out_shape=jax.ShapeDtypeStruct((nrows, k), BF16),
        compiler_params=_params(("parallel",)),
        name="cast_rows",
    )(w)


def _mla_prep_kernel(cq_ref, ckv_ref, kr_ref, krrot_ref, pos_ref, invf_ref, gq_ref, gkv_ref,
                     wq_ref, wkv_ref, q_ref, k_ref, v_ref, cnq_s, cnkv_s, krope_s, cos_s, sin_s):
    @pl.when(pl.program_id(1) == 0)
    def _():
        ang = pos_ref[...] * invf_ref[...]
        cos, sin = jnp.cos(ang), jnp.sin(ang)
        cos_s[...] = cos
        sin_s[...] = sin
        cnq_s[...] = _rms(cq_ref[...], gq_ref[...]).astype(BF16)
        cnkv_s[...] = _rms(ckv_ref[...], gkv_ref[...]).astype(BF16)
        krope_s[...] = (kr_ref[...] * cos + krrot_ref[...] * sin).astype(BF16)

    lane = lax.broadcasted_iota(jnp.int32, (v_ref.shape[1], LANES), 1)
    ones_col = (lane == 0).astype(BF16)
    for hh in range(q_ref.shape[0]):
        qh = jnp.dot(cnq_s[...], wq_ref[hh], preferred_element_type=F32)
        q_rope = qh[:, LANES:2 * LANES] * cos_s[...] + qh[:, 2 * LANES:] * sin_s[...]
        q_ref[hh, :, :LANES] = qh[:, :LANES].astype(BF16)
        q_ref[hh, :, LANES:] = q_rope.astype(BF16)
        kvh = jnp.dot(cnkv_s[...], wkv_ref[hh], preferred_element_type=F32)
        k_ref[hh, :, :LANES] = kvh[:, :LANES].astype(BF16)
        k_ref[hh, :, LANES:] = krope_s[...]
        v_ref[hh, :, :LANES] = kvh[:, LANES:].astype(BF16)
        v_ref[hh, :, LANES:] = ones_col


PREP_HEADS = 4


def _mla_prep(p1, pos, invf, gq, gkv, wq, wkv, tm=512):
    s = p1.shape[0]
    tm = min(tm, s)
    hb = PREP_HEADS
    row = lambda w, c: pl.BlockSpec((tm, w), lambda i, h, c=c: (i, c))
    const = lambda shape: pl.BlockSpec(shape, lambda i, h: (0,) * len(shape))
    head_out = lambda w: pl.BlockSpec((hb, tm, w), lambda i, h: (h, i, 0))
    return pl.pallas_call(
        _mla_prep_kernel,
        grid=(s // tm, A_HEADS // hb),
        in_specs=[
            row(A_Q_RANK, P1_CQ // A_Q_RANK), row(A_KV_RANK, P1_CKV // A_KV_RANK),
            row(LANES, P1_KR // LANES), row(LANES, P1_KRROT // LANES),
            pl.BlockSpec((tm, 1), lambda i, h: (i, 0)), const((1, LANES)),
            const((1, A_Q_RANK)), const((1, A_KV_RANK)),
            pl.BlockSpec((hb, A_Q_RANK, 3 * LANES), lambda i, h: (h, 0, 0)),
            pl.BlockSpec((hb, A_KV_RANK, 2 * LANES), lambda i, h: (h, 0, 0)),
        ],
        out_specs=[head_out(A_QK), head_out(A_QK), head_out(A_VEXT)],
        out_shape=[jax.ShapeDtypeStruct((A_HEADS, s, A_QK), BF16),
                   jax.ShapeDtypeStruct((A_HEADS, s, A_QK), BF16),
                   jax.ShapeDtypeStruct((A_HEADS, s, A_VEXT), BF16)],
        scratch_shapes=[pltpu.VMEM((tm, A_Q_RANK), BF16), pltpu.VMEM((tm, A_KV_RANK), BF16),
                        pltpu.VMEM((tm, LANES), BF16), pltpu.VMEM((tm, LANES), F32),
                        pltpu.VMEM((tm, LANES), F32)],
        compiler_params=_params(("parallel", "arbitrary")),
        name="mla_prep",
    )(p1, p1, p1, p1, pos, invf, gq, gkv, wq, wkv)


def _attn_kernel(q_ref, k_ref, v_ref, gate_ref, o_ref, m_s, alpha_s, acc_s, sa_s, sb_s):
    kv = pl.program_id(2)
    tq = acc_s.shape[0]
    rc, tk = sa_s.shape
    nrc = tq // rc

    @pl.when(kv == 0)
    def _():
        m_s[...] = jnp.full(m_s.shape, -jnp.inf, F32)
        acc_s[...] = jnp.zeros(acc_s.shape, F32)

    def scores(c, s_ref):
        rows = pl.ds(c * rc, rc)
        s = lax.dot_general(q_ref[0, rows, :], k_ref[0], (((1,), (1,)), ((), ())), preferred_element_type=F32)
        s_ref[...] = s
        m_prev = m_s[rows, :]
        m_new = jnp.maximum(m_prev, jnp.max(s, axis=-1, keepdims=True))
        alpha_s[rows, :] = jnp.exp2(m_prev - m_new)
        m_s[rows, :] = m_new

    def accumulate(c, s_ref):
        rows = pl.ds(c * rc, rc)
        p = jnp.exp2(s_ref[...] - _lane_tile(m_s[rows, :], tk // LANES))
        acc_s[rows, :] = _lane_tile(alpha_s[rows, :], A_VEXT // LANES) * acc_s[rows, :] + jnp.dot(
            p.astype(BF16), v_ref[0], preferred_element_type=F32)

    bufs = (sa_s, sb_s)
    scores(0, bufs[0])
    for c in range(nrc):
        accumulate(c, bufs[c % 2])
        if c + 1 < nrc:
            scores(c + 1, bufs[(c + 1) % 2])

    @pl.when(kv == pl.num_programs(2) - 1)
    def _():
        acc = acc_s[...]
        l = jnp.max(acc[:, A_VDIM:], axis=-1, keepdims=True)
        o_ref[...] = (acc[:, :A_VDIM] / l * _silu(gate_ref[...].astype(F32))).astype(o_ref.dtype)


def _attention(q, k, v, p2, tq=4096, tk=2048):
    h, s, _ = q.shape
    tq, tk = min(tq, s), min(tk, s)
    rc = min(ATTN_ROWS, tq)
    return pl.pallas_call(
        _attn_kernel,
        grid=(h, s // tq, s // tk),
        in_specs=[
            pl.BlockSpec((1, tq, A_QK), lambda h, i, j: (h, i, 0)),
            pl.BlockSpec((1, tk, A_QK), lambda h, i, j: (h, j, 0)),
            pl.BlockSpec((1, tk, A_VEXT), lambda h, i, j: (h, j, 0)),
            pl.BlockSpec((tq, A_VDIM), lambda h, i, j: (i, P2_AGATE // A_VDIM + h)),
        ],
        out_specs=pl.BlockSpec((tq, A_VDIM), lambda h, i, j: (i, h)),
        out_shape=jax.ShapeDtypeStruct((s, A_HEADS * A_VDIM), BF16),
        scratch_shapes=[pltpu.VMEM((tq, LANES), F32), pltpu.VMEM((tq, LANES), F32), pltpu.VMEM((tq, A_VEXT), F32),
                        pltpu.VMEM((rc, tk), F32), pltpu.VMEM((rc, tk), F32)],
        compiler_params=_params(("parallel", "parallel", "arbitrary")),
        name="mla_attention",
    )(q, k, v, p2)


CONV_CW = 128
CONV_ROWS = 128
CONV_OFF = CONV_HALO - CONV_W // 2
CONV_REACH = (CONV_OFF + CONV_W - 1) // SUBLANES * SUBLANES


def _conv_kernel(a_ref, g_ref, ap_ref, gp_ref, an_ref, gn_ref, bg_ref, w_ref, cb_ref, lg_ref, lb_ref,
                 o_ref, u_s, c_s, sh_s):
    i = pl.program_id(0)
    tm = a_ref.shape[0]
    nchunk = BRANCH_W // CONV_CW
    glu = lambda a, g: a.astype(F32) * jax.nn.sigmoid(g.astype(F32))
    first, last = i > 0, i < pl.num_programs(0) - 1
    for ci in range(nchunk):
        cs = slice(ci * CONV_CW, (ci + 1) * CONV_CW)
        u_s[ci, CONV_HALO:CONV_HALO + tm, :] = glu(a_ref[:, cs], g_ref[:, cs])
        u_s[ci, :CONV_HALO, :] = jnp.where(first, glu(ap_ref[:, cs], gp_ref[:, cs]), 0.0)
        u_s[ci, CONV_HALO + tm:, :] = jnp.where(last, glu(an_ref[:, cs], gn_ref[:, cs]), 0.0)

    off, reach = CONV_OFF, CONV_REACH
    rows = min(CONV_ROWS, tm)

    def chunk(ci, carry):
        for r0 in range(0, tm, rows):
            acc = jnp.broadcast_to(cb_ref[ci], (rows, CONV_CW))
            for b in range(SUBLANES):
                sh_s[b] = u_s[ci, r0 + b:r0 + b + rows + reach, :]
                for t in range(CONV_W):
                    if (off + t) % SUBLANES == b:
                        a = off + t - b
                        acc = acc + sh_s[b, a:a + rows, :] * w_ref[ci, t:t + 1, :]
            c_s[ci, r0:r0 + rows, :] = acc
        return carry

    lax.fori_loop(0, nchunk, chunk, 0)

    total = c_s[0]
    for ci in range(1, nchunk):
        total = total + c_s[ci]
    mean = jnp.sum(total, axis=-1, keepdims=True) * (1.0 / BRANCH_W)
    sq = jnp.zeros_like(total)
    for ci in range(nchunk):
        d = c_s[ci] - mean
        sq = sq + d * d
    rstd = lax.rsqrt(jnp.sum(sq, axis=-1, keepdims=True) * (1.0 / BRANCH_W) + EPS)
    for ci in range(nchunk):
        cs = slice(ci * CONV_CW, (ci + 1) * CONV_CW)
        y = (c_s[ci] - mean) * rstd * lg_ref[:, cs] + lb_ref[:, cs]
        o_ref[:, cs] = (_silu(y) * _silu(bg_ref[:, cs].astype(F32))).astype(o_ref.dtype)


def _conv_module(p2, w, cb, lg, lb, tm=256):
    s = p2.shape[0]
    tm = min(tm, s)
    r = tm // CONV_HALO
    nh = s // CONV_HALO
    wide = lambda c: pl.BlockSpec((tm, BRANCH_W), lambda i, c=c: (i, c))
    prev = lambda c: pl.BlockSpec((CONV_HALO, BRANCH_W), lambda i, c=c: (jnp.maximum(i * r - 1, 0), c))
    nxt = lambda c: pl.BlockSpec((CONV_HALO, BRANCH_W), lambda i, c=c: (jnp.minimum((i + 1) * r, nh - 1), c))
    const = lambda rows: pl.BlockSpec((rows, BRANCH_W), lambda i: (0, 0))
    nchunk = BRANCH_W // CONV_CW
    chunked = lambda rows: pl.BlockSpec((nchunk, rows, CONV_CW), lambda i: (0, 0, 0))
    ca, cg, cbg = P2_GLU_A // BRANCH_W, P2_GLU_G // BRANCH_W, P2_BGATE // BRANCH_W
    w_c = jnp.transpose(w.reshape(w.shape[0], nchunk, CONV_CW), (1, 0, 2))
    cb_c = cb.reshape(nchunk, 1, CONV_CW)
    return pl.pallas_call(
        _conv_kernel,
        grid=(s // tm,),
        in_specs=[wide(ca), wide(cg), prev(ca), prev(cg), nxt(ca), nxt(cg), wide(cbg),
                  chunked(w.shape[0]), chunked(1), const(1), const(1)],
        out_specs=pl.BlockSpec((tm, BRANCH_W), lambda i: (i, 0)),
        out_shape=jax.ShapeDtypeStruct((s, BRANCH_W), BF16),
        scratch_shapes=[pltpu.VMEM((nchunk, tm + 2 * CONV_HALO, CONV_CW), F32),
                        pltpu.VMEM((nchunk, tm, CONV_CW), F32),
                        pltpu.VMEM((SUBLANES, min(CONV_ROWS, tm) + CONV_REACH, CONV_CW), F32)],
        compiler_params=_params(("parallel",)),
        name="conv_module",
    )(p2, p2, p2, p2, p2, p2, p2, w_c, cb_c, lg, lb)


def _mlstm_kernel(qf_ref, kf_ref, vf_ref, gf_ref, qb_ref, kb_ref, vb_ref, gb_ref, bias_ref,
                  hf_ref, hb_ref, c_s, n_s, m_s):
    L = qf_ref.shape[0]

    @pl.when(pl.program_id(0) == 0)
    def _():
        c_s[...] = jnp.zeros(c_s.shape, F32)
        n_s[...] = jnp.zeros(n_s.shape, F32)
        m_s[...] = jnp.zeros(m_s.shape, F32)

    rows = lax.broadcasted_iota(jnp.int32, (L, L), 0)
    cols = lax.broadcasted_iota(jnp.int32, (L, L), 1)
    lower = rows >= cols
    upper = rows <= cols
    tri_l = lower.astype(F32)
    tri_u = upper.astype(F32)
    hi = lax.Precision.HIGHEST

    for d, (q_ref, k_ref, v_ref, g_ref, h_ref) in enumerate(
            ((qf_ref, kf_ref, vf_ref, gf_ref, hf_ref), (qb_ref, kb_ref, vb_ref, gb_ref, hb_ref))):
        g = g_ref[...] + bias_ref[...]
        g_t = g.T
        lf = jax.nn.log_sigmoid(g)
        lf_t = jax.nn.log_sigmoid(g_t)
        mask = lower if d == 0 else upper
        bcum = jnp.dot(tri_l if d == 0 else tri_u, lf, precision=hi, preferred_element_type=F32)
        bcum_t = jnp.dot(lf_t, tri_u if d == 0 else tri_l, precision=hi, preferred_element_type=F32)
        last = L - 1 if d == 0 else 0
        for h in range(C_HEADS):
            ci, cf, sidx = d * 8 + h, d * 8 + 4 + h, d * C_HEADS + h
            qh = q_ref[:, h * C_DK:(h + 1) * C_DK]
            kh = k_ref[:, h * C_DK:(h + 1) * C_DK] * (C_DK ** -0.5)
            vh = v_ref[:, h * C_DV:(h + 1) * C_DV]
            bcol, brow = bcum[:, cf:cf + 1], bcum_t[cf:cf + 1, :]
            icol, irow = g[:, ci:ci + 1], g_t[ci:ci + 1, :]
            m_prev = m_s[sidx]
            c_prev = c_s[sidx]
            n_prev = n_s[sidx]

            logw = jnp.where(mask, bcol - brow + irow, -jnp.inf)
            inter = bcol + m_prev
            m_loc = jnp.maximum(inter, jnp.max(logw, axis=-1, keepdims=True))
            w_intra = jnp.exp(logw - m_loc)
            w_inter = jnp.exp(inter - m_loc)
            s = lax.dot_general(qh, kh, (((1,), (1,)), ((), ())), preferred_element_type=F32) * w_intra
            num = (jnp.dot(s.astype(BF16), vh, preferred_element_type=F32)
                   + w_inter * jnp.dot(qh, c_prev.astype(BF16), preferred_element_type=F32))
            den = (jnp.sum(s, axis=-1, keepdims=True)
                   + w_inter * jnp.sum(qh.astype(F32) * n_prev, axis=-1, keepdims=True))
            h_ref[:, h * C_DV:(h + 1) * C_DV] = num / jnp.maximum(jnp.abs(den), jnp.exp(-m_loc))

            b_tot = bcol[last:last + 1, :]
            lw_end = b_tot - bcol + icol
            m_new = jnp.maximum(b_tot + m_prev, jnp.max(lw_end, axis=0, keepdims=True))
            decay = jnp.exp(b_tot + m_prev - m_new)
            wk = jnp.exp(lw_end - m_new) * kh.astype(F32)
            c_s[sidx] = decay * c_prev + lax.dot_general(
                wk.astype(BF16), vh, (((0,), (0,)), ((), ())), preferred_element_type=F32)
            n_s[sidx] = decay * n_prev + jnp.sum(wk, axis=0, keepdims=True)
            m_s[sidx] = m_new


def _mlstm(p1, p2, bias):
    s = p2.shape[0]
    L = min(MLSTM_CHUNK, s)
    nc = s // L
    wq, wv = C_HEADS * C_DK, C_HEADS * C_DV
    fwd = lambda w, c: pl.BlockSpec((L, w), lambda i, c=c: (i, c))
    bwd = lambda w, c: pl.BlockSpec((L, w), lambda i, c=c: (nc - 1 - i, c))
    specs = lambda mk: [mk(wq, P2_CQ // wq), mk(wq, P2_CK // wq), mk(wv, P2_CV // wv)]
    return pl.pallas_call(
        _mlstm_kernel,
        grid=(nc,),
        in_specs=specs(fwd) + [fwd(LANES, P1_CIF // LANES)] + specs(bwd) + [bwd(LANES, P1_CIF // LANES)]
        + [pl.BlockSpec((1, LANES), lambda i: (0, 0))],
        out_specs=[pl.BlockSpec((L, wv), lambda i: (i, 0)), pl.BlockSpec((L, wv), lambda i: (nc - 1 - i, 0))],
        out_shape=[jax.ShapeDtypeStruct((s, wv), F32), jax.ShapeDtypeStruct((s, wv), F32)],
        scratch_shapes=[pltpu.VMEM((2 * C_HEADS, C_DK, C_DV), F32), pltpu.VMEM((2 * C_HEADS, 1, C_DK), F32),
                        pltpu.VMEM((2 * C_HEADS, 1, 1), F32)],
        compiler_params=_params(("arbitrary",)),
        name="mlstm_scan",
    )(p2, p2, p2, p1, p2, p2, p2, p1, bias)


def _mlstm_out_kernel(hf_ref, hb_ref, o_ref, gate_ref, ng_ref, y_ref):
    for h in range(C_HEADS):
        sl = slice(h * C_DV, (h + 1) * C_DV)
        x = hf_ref[:, sl] + hb_ref[:, sl]
        x = x * lax.rsqrt(jnp.mean(x * x, axis=-1, keepdims=True) + EPS) * ng_ref[:, sl]
        x = jax.nn.sigmoid(o_ref[:, sl].astype(F32)) * x
        y_ref[:, sl] = (x * _silu(gate_ref[:, sl].astype(F32))).astype(y_ref.dtype)


def _mlstm_out(hf, hb, p2, ng, tm=256):
    s, w = hf.shape
    tm = min(tm, s)
    row = lambda c: pl.BlockSpec((tm, w), lambda i, c=c: (i, c))
    return pl.pallas_call(
        _mlstm_out_kernel,
        grid=(s // tm,),
        in_specs=[row(0), row(0), row(P2_CO // w), row(P2_CGATE // w), pl.BlockSpec((1, w), lambda i: (0, 0))],
        out_specs=row(0),
        out_shape=jax.ShapeDtypeStruct((s, w), BF16),
        compiler_params=_params(("parallel",)),
        name="mlstm_out",
    )(hf, hb, p2, p2, ng)


def _mem_kv_kernel(mem_ref, g_ref, w_ref, o_ref):
    hn = _rms(mem_ref[...], g_ref[...]).astype(BF16)
    o_ref[...] = jnp.dot(hn, w_ref[...], preferred_element_type=F32).astype(o_ref.dtype)


def _mem_kv(mem, g, w):
    nm, d = mem.shape
    n = w.shape[1]
    tn = 512
    return pl.pallas_call(
        _mem_kv_kernel,
        grid=(n // tn,),
        in_specs=[pl.BlockSpec((nm, d), lambda j: (0, 0)), pl.BlockSpec((1, d), lambda j: (0, 0)),
                  pl.BlockSpec((d, tn), lambda j: (0, j))],
        out_specs=pl.BlockSpec((nm, tn), lambda j: (0, j)),
        out_shape=jax.ShapeDtypeStruct((nm, n), BF16),
        compiler_params=_params(("parallel",)),
        name="mem_kv",
    )(mem, g, w)


def _mem_attn_kernel(q_ref, kv_ref, o_ref):
    scale = D_HDIM ** -0.5
    for h in range(D_HEADS):
        sl = slice(h * D_HDIM, (h + 1) * D_HDIM)
        q = (q_ref[:, sl] * scale).astype(BF16)
        s = lax.dot_general(q, kv_ref[:, sl], (((1,), (1,)), ((), ())), preferred_element_type=F32)
        p = jnp.exp(s - jnp.max(s, axis=-1, keepdims=True))
        l = jnp.sum(p, axis=-1, keepdims=True)
        v = kv_ref[:, D_W + h * D_HDIM:D_W + (h + 1) * D_HDIM]
        o_ref[:, sl] = (jnp.dot(p.astype(BF16), v, preferred_element_type=F32) / l).astype(o_ref.dtype)


def _mem_attention(p1, kv, tm=512):
    s = p1.shape[0]
    tm = min(tm, s)
    nm = kv.shape[0]
    return pl.pallas_call(
        _mem_attn_kernel,
        grid=(s // tm,),
        in_specs=[pl.BlockSpec((tm, D_W), lambda i: (i, P1_DQ // D_W)),
                  pl.BlockSpec((nm, 2 * D_W), lambda i: (0, 0))],
        out_specs=pl.BlockSpec((tm, D_W), lambda i: (i, 0)),
        out_shape=jax.ShapeDtypeStruct((s, D_W), BF16),
        compiler_params=_params(("parallel",)),
        name="mem_attention",
    )(p1, kv)


def _merge_kernel(h_ref, ya_ref, yb_ref, yc_ref, yd_ref, wm0, wm1, wm2, wm3, wa, wb, wc, wd, z_ref):
    h = h_ref[...]
    z = None
    for y_ref, wm, wp in ((ya_ref, wm0, wa), (yb_ref, wm1, wb), (yc_ref, wm2, wc), (yd_ref, wm3, wd)):
        gate = jax.nn.sigmoid(lax.dot_general(h, wm[...], NT_DIMS, preferred_element_type=F32))
        term = gate * jnp.dot(y_ref[...], wp[...], preferred_element_type=F32)
        z = term if z is None else z + term
    z_ref[...] = z.astype(z_ref.dtype)


def _merge(h, ya, yb, yc, yd, wm_t, wa, wb, wc, wd, tm=512, tn=256):
    s, d = h.shape
    tm = min(tm, s)
    nj = d // tn
    row = lambda w: pl.BlockSpec((tm, w), lambda j, i: (i, 0))
    col = lambda k: pl.BlockSpec((k, tn), lambda j, i: (0, j))
    gate_w = lambda b: pl.BlockSpec((tn, d), lambda j, i, b=b: (b * nj + j, 0))
    return pl.pallas_call(
        _merge_kernel,
        grid=(nj, s // tm),
        in_specs=[row(d), row(ya.shape[1]), row(yb.shape[1]), row(yc.shape[1]), row(yd.shape[1]),
                  gate_w(0), gate_w(1), gate_w(2), gate_w(3),
                  col(wa.shape[0]), col(wb.shape[0]), col(wc.shape[0]), col(wd.shape[0])],
        out_specs=pl.BlockSpec((tm, tn), lambda j, i: (i, j)),
        out_shape=jax.ShapeDtypeStruct((s, d), BF16),
        compiler_params=_params(("parallel", "parallel")),
        name="gated_merge",
    )(h, ya, yb, yc, yd, wm_t, wm_t, wm_t, wm_t, wa, wb, wc, wd)


def _pack_in_proj(w_t, layer):
    d = w_t.shape[2]
    rows = lambda a, b: lax.slice(w_t, (layer, a, 0), (layer + 1, b, d))[0]
    kr = rows(P1_CKV + A_KV_RANK, IN_AGATE)
    half = A_ROPE // 2
    kr_rot = jnp.concatenate([-kr[half:], kr[:half]], axis=0)
    zeros = lambda n: jnp.zeros((n, d), w_t.dtype)
    w1 = jnp.concatenate([
        rows(0, P1_CKV + A_KV_RANK), rows(IN_DQ, IN_MERGE),
        kr, zeros(LANES - A_ROPE), kr_rot, zeros(LANES - A_ROPE),
        rows(IN_CIF, IN_DQ), zeros(P1_W - P1_CIF - (IN_DQ - IN_CIF))], axis=0)
    w1 = _cast_rows(w1, 0, P1_W)
    n_in = w_t.shape[1]
    w_flat = w_t.reshape(-1, d)
    w2 = _cast_rows(w_flat, layer * n_in + IN_AGATE, IN_CIF - IN_AGATE)
    wm = _cast_rows(w_flat, layer * n_in + IN_MERGE, n_in - IN_MERGE)
    return w1, w2, wm


def _pack_uq(w_uq):
    r = w_uq.shape[0]
    w = w_uq.reshape(r, A_HEADS, A_NOPE + A_ROPE)
    nope, rope = w[..., :A_NOPE], w[..., A_NOPE:]
    half = A_ROPE // 2
    rot = jnp.concatenate([-rope[..., half:], rope[..., :half]], axis=-1)
    z = jnp.zeros((r, A_HEADS, LANES - A_ROPE), w.dtype)
    packed = jnp.concatenate([nope, rope, z, rot, z], axis=-1)
    return jnp.transpose(packed, (1, 0, 2)).astype(BF16)


def _pack_ukv(w_ukv):
    r = w_ukv.shape[0]
    w = w_ukv.reshape(r, A_HEADS, A_NOPE + A_VDIM)
    return jnp.transpose(w, (1, 0, 2)).astype(BF16)


def _layer(x, mem, pos, invf, norm_g, w_in, layer, cq_g, w_uq, ckv_g, w_ukv, conv_w, conv_b, ln_g, ln_b,
           gate_b, mlstm_g, mem_g, w_mem_kv, w_pa, w_pb, w_pc, w_pd, w_out):
    w1, w2, wm = _pack_in_proj(w_in, layer)
    h = _rmsnorm(x, norm_g, BF16)
    p1 = _matmul(h, w1, F32, tm=1024, tn=1280, transposed_w=True, name="in_proj_f32")
    p2 = _matmul(h, w2, BF16, tm=1024, tn=1024, transposed_w=True, name="in_proj_bf16")

    q_gain = cq_g * ((A_NOPE + A_ROPE) ** -0.5 * LOG2_E)
    q, k, v = _mla_prep(p1, pos, invf, q_gain.reshape(1, -1), ckv_g.reshape(1, -1), _pack_uq(w_uq), _pack_ukv(w_ukv))
    y_a = _attention(q, k, v, p2)

    conv_w_pad = jnp.concatenate([conv_w, jnp.zeros((1, BRANCH_W), F32)], axis=0)
    y_b = _conv_module(p2, conv_w_pad, conv_b.reshape(1, -1), ln_g.reshape(1, -1), ln_b.reshape(1, -1))

    bias = jnp.zeros((1, LANES), F32).at[0, :4 * C_HEADS].set(gate_b.reshape(-1))
    hf, hb = _mlstm(p1, p2, bias)
    y_c = _mlstm_out(hf, hb, p2, mlstm_g.reshape(1, -1))

    kv = _mem_kv(mem, mem_g.reshape(1, -1), w_mem_kv.astype(BF16))
    y_d = _mem_attention(p1, kv)

    z = _merge(h, y_a, y_b, y_c, y_d, wm, w_pa.astype(BF16), w_pb.astype(BF16), w_pc.astype(BF16),
               w_pd.astype(BF16))
    return _matmul(z, w_out.astype(BF16), F32, tm=1024, tn=1024, residual=x, name="out_proj")


def kernel(x, mem, positions, norm_g, w_in, mla_cq_norm_g, w_uq, mla_ckv_norm_g, w_ukv, conv_w, conv_b,
           conv_ln_g, conv_ln_b, mlstm_gate_b, mlstm_norm_g, mem_norm_g, w_mem_kv, w_proj_a, w_proj_b,
           w_proj_c, w_proj_d, w_out, final_norm_g):
    b, s, d = x.shape
    depth = w_in.shape[0]
    inv_freq = ROPE_THETA ** (-jnp.arange(0, A_ROPE, 2, dtype=F32) / A_ROPE)
    invf = jnp.tile(inv_freq, LANES // (A_ROPE // 2)).reshape(1, LANES)
    w_in = jnp.swapaxes(w_in, 1, 2)
    outs = []
    for bi in range(b):
        xb = x[bi]
        pos = positions[bi].astype(F32).reshape(s, 1)
        for l in range(depth):
            xb = _layer(xb, mem[bi], pos, invf, norm_g[l], w_in, l, mla_cq_norm_g[l], w_uq[l],
                        mla_ckv_norm_g[l], w_ukv[l], conv_w[l], conv_b[l], conv_ln_g[l], conv_ln_b[l],
                        mlstm_gate_b[l], mlstm_norm_g[l], mem_norm_g[l], w_mem_kv[l], w_proj_a[l],
                        w_proj_b[l], w_proj_c[l], w_proj_d[l], w_out[l])
        outs.append(_rmsnorm(xb, final_norm_g, x.dtype))
    return jnp.stack(outs, axis=0)
```

```python
import functools

import jax
import jax.numpy as jnp
from jax import lax
from jax.experimental import pallas as pl
from jax.experimental.pallas import tpu as pltpu

F32 = jnp.float32
BF16 = jnp.bfloat16

D_MODEL = 4096
BRANCH_W = D_MODEL // 2
A_HEADS = 16
A_NOPE = 128
A_ROPE = 64
A_VDIM = 128
A_Q_RANK = 1024
A_KV_RANK = 512
ROPE_THETA = 10000.0
CONV_W = 31
C_HEADS = 4
C_DK = 256
C_DV = 512
D_HEADS = 4
D_HDIM = 128
D_W = D_HEADS * D_HDIM
N_BRANCH = 4
EPS = 1e-6
LOG2_E = 1.4426950408889634

LANES = 128
SUBLANES = 8
A_QK = 2 * LANES
A_VEXT = 2 * LANES
ATTN_ROWS = 256
MLSTM_CHUNK = 256
CONV_HALO = 16
VMEM_LIMIT = 56 * 1024 * 1024

P1_CQ, P1_CKV, P1_DQ, P1_KR, P1_KRROT, P1_CIF, P1_W = 0, 1024, 1536, 2048, 2176, 2304, 2560
P2_AGATE, P2_GLU_A, P2_GLU_G, P2_BGATE = 0, 2048, 4096, 6144
P2_CQ, P2_CK, P2_CV, P2_CO, P2_CGATE, P2_W = 8192, 9216, 10240, 12288, 14336, 16384
IN_AGATE, IN_CIF, IN_DQ, IN_MERGE = 1600, 17984, 18000, 18512


def _params(semantics):
    return pltpu.CompilerParams(dimension_semantics=semantics, vmem_limit_bytes=VMEM_LIMIT)


def _silu(x):
    return x * jax.nn.sigmoid(x)


def _lane_tile(x, n):
    return jnp.concatenate([x] * n, axis=1)


def _rms(x, g):
    return x * lax.rsqrt(jnp.mean(x * x, axis=-1, keepdims=True) + EPS) * g


def _rmsnorm_kernel(x_ref, g_ref, o_ref):
    o_ref[...] = _rms(x_ref[...], g_ref[...]).astype(o_ref.dtype)


def _rmsnorm(x, g, out_dtype, tm=512):
    m, d = x.shape
    tm = min(tm, m)
    return pl.pallas_call(
        _rmsnorm_kernel,
        grid=(m // tm,),
        in_specs=[pl.BlockSpec((tm, d), lambda i: (i, 0)), pl.BlockSpec((1, d), lambda i: (0, 0))],
        out_specs=pl.BlockSpec((tm, d), lambda i: (i, 0)),
        out_shape=jax.ShapeDtypeStruct((m, d), out_dtype),
        compiler_params=_params(("parallel",)),
        name="rmsnorm",
    )(x, g.reshape(1, d))


NT_DIMS = (((1,), (1,)), ((), ()))


def _mm_kernel(a_ref, w_ref, o_ref):
    o_ref[...] = jnp.dot(a_ref[...], w_ref[...], preferred_element_type=F32).astype(o_ref.dtype)


def _mm_nt_kernel(a_ref, w_ref, o_ref):
    o_ref[...] = lax.dot_general(a_ref[...], w_ref[...], NT_DIMS, preferred_element_type=F32).astype(o_ref.dtype)


def _mm_res_kernel(a_ref, w_ref, r_ref, o_ref):
    acc = jnp.dot(a_ref[...], w_ref[...], preferred_element_type=F32)
    o_ref[...] = (r_ref[...] + acc).astype(o_ref.dtype)


def _matmul(a, w, out_dtype, *, tm, tn, residual=None, transposed_w=False, name="matmul"):
    m, k = a.shape
    n = w.shape[0] if transposed_w else w.shape[1]
    tm, tn = min(tm, m), min(tn, n)
    w_spec = (pl.BlockSpec((tn, k), lambda j, i: (j, 0)) if transposed_w
              else pl.BlockSpec((k, tn), lambda j, i: (0, j)))
    in_specs = [pl.BlockSpec((tm, k), lambda j, i: (i, 0)), w_spec]
    args = [a, w]
    kern = _mm_nt_kernel if transposed_w else _mm_kernel
    if residual is not None:
        assert not transposed_w
        in_specs.append(pl.BlockSpec((tm, tn), lambda j, i: (i, j)))
        args.append(residual)
        kern = _mm_res_kernel
    return pl.pallas_call(
        kern,
        grid=(n // tn, m // tm),
        in_specs=in_specs,
        out_specs=pl.BlockSpec((tm, tn), lambda j, i: (i, j)),
        out_shape=jax.ShapeDtypeStruct((m, n), out_dtype),
        compiler_params=_params(("parallel", "parallel")),
        name=name,
    )(*args)


def _cast_rows_kernel(w_ref, o_ref):
    o_ref[...] = w_ref[...].astype(o_ref.dtype)


def _cast_rows(w, start, nrows, tr=512):
    k = w.shape[1]
    tr = min(tr, nrows)
    assert start % SUBLANES == 0 and nrows % tr == 0 and tr % SUBLANES == 0
    return pl.pallas_call(
        _cast_rows_kernel,
        grid=(nrows // tr,),
        in_specs=[pl.BlockSpec((pl.Element(tr), pl.Element(k)),
                               lambda i: (pl.multiple_of(start + i * tr, SUBLANES), 0))],
        out_specs=pl.BlockSpec((tr, k), lambda i: (i, 0)),
        out_shape=jax.ShapeDtypeStruct((nrows, k), BF16),
        compiler_params=_params(("parallel",)),
        name="cast_rows",
    )(w)


def _mla_prep_kernel(cq_ref, ckv_ref, kr_ref, krrot_ref, pos_ref, invf_ref, gq_ref, gkv_ref,
                     wq_ref, wkv_ref, q_ref, k_ref, v_ref, cnq_s, cnkv_s, krope_s, cs_s):
    @pl.when(pl.program_id(1) == 0)
    def _():
        ang = pos_ref[...] * invf_ref[...]
        cos, sin = jnp.cos(ang), jnp.sin(ang)
        lane = lax.broadcasted_iota(jnp.int32, ang.shape, 1)
        cs_s[...] = jnp.where(lane < A_ROPE, cos, sin)
        cnq_s[...] = _rms(cq_ref[...], gq_ref[...]).astype(BF16)
        cnkv_s[...] = _rms(ckv_ref[...], gkv_ref[...]).astype(BF16)
        krope_s[...] = (kr_ref[...] * cos + krrot_ref[...] * sin).astype(BF16)

    lane = lax.broadcasted_iota(jnp.int32, (v_ref.shape[1], LANES), 1)
    ones_col = (lane == 0).astype(BF16)
    for hh in range(q_ref.shape[0]):
        qh = jnp.dot(cnq_s[...], wq_ref[hh], preferred_element_type=F32)
        t = qh[:, LANES:] * cs_s[...]
        q_rope = t + pltpu.roll(t, A_ROPE, axis=1)
        q_ref[hh, :, :LANES] = qh[:, :LANES].astype(BF16)
        q_ref[hh, :, LANES:] = q_rope.astype(BF16)
        kvh = jnp.dot(cnkv_s[...], wkv_ref[hh], preferred_element_type=F32)
        k_ref[hh, :, :LANES] = kvh[:, :LANES].astype(BF16)
        k_ref[hh, :, LANES:] = krope_s[...]
        v_ref[hh, :, :LANES] = kvh[:, LANES:].astype(BF16)
        v_ref[hh, :, LANES:] = ones_col


PREP_HEADS = 4


def _mla_prep(p1, pos, invf, gq, gkv, wq, wkv, tm=512):
    s = p1.shape[0]
    tm = min(tm, s)
    hb = PREP_HEADS
    row = lambda w, c: pl.BlockSpec((tm, w), lambda i, h, c=c: (i, c))
    const = lambda shape: pl.BlockSpec(shape, lambda i, h: (0,) * len(shape))
    head_out = lambda w: pl.BlockSpec((hb, tm, w), lambda i, h: (h, i, 0))
    return pl.pallas_call(
        _mla_prep_kernel,
        grid=(s // tm, A_HEADS // hb),
        in_specs=[
            row(A_Q_RANK, P1_CQ // A_Q_RANK), row(A_KV_RANK, P1_CKV // A_KV_RANK),
            row(LANES, P1_KR // LANES), row(LANES, P1_KRROT // LANES),
            pl.BlockSpec((tm, 1), lambda i, h: (i, 0)), const((1, LANES)),
            const((1, A_Q_RANK)), const((1, A_KV_RANK)),
            pl.BlockSpec((hb, A_Q_RANK, 2 * LANES), lambda i, h: (h, 0, 0)),
            pl.BlockSpec((hb, A_KV_RANK, 2 * LANES), lambda i, h: (h, 0, 0)),
        ],
        out_specs=[head_out(A_QK), head_out(A_QK), head_out(A_VEXT)],
        out_shape=[jax.ShapeDtypeStruct((A_HEADS, s, A_QK), BF16),
                   jax.ShapeDtypeStruct((A_HEADS, s, A_QK), BF16),
                   jax.ShapeDtypeStruct((A_HEADS, s, A_VEXT), BF16)],
        scratch_shapes=[pltpu.VMEM((tm, A_Q_RANK), BF16), pltpu.VMEM((tm, A_KV_RANK), BF16),
                        pltpu.VMEM((tm, LANES), BF16), pltpu.VMEM((tm, LANES), F32)],
        compiler_params=_params(("parallel", "arbitrary")),
        name="mla_prep",
    )(p1, p1, p1, p1, pos, invf, gq, gkv, wq, wkv)


def _attn_kernel(q_ref, k_ref, v_ref, gate_ref, o_ref, m_s, alpha_s, acc_s, sa_s, sb_s):
    kv = pl.program_id(2)
    tq = acc_s.shape[0]
    rc, tk = sa_s.shape
    nrc = tq // rc

    @pl.when(kv == 0)
    def _():
        m_s[...] = jnp.full(m_s.shape, -jnp.inf, F32)
        acc_s[...] = jnp.zeros(acc_s.shape, F32)

    def scores(c, s_ref):
        rows = pl.ds(c * rc, rc)
        s = lax.dot_general(q_ref[0, rows, :], k_ref[0], (((1,), (1,)), ((), ())), preferred_element_type=F32)
        s_ref[...] = s
        m_prev = m_s[rows, :]
        m_new = jnp.maximum(m_prev, jnp.max(s, axis=-1, keepdims=True))
        alpha_s[rows, :] = jnp.exp2(m_prev - m_new)
        m_s[rows, :] = m_new

    def accumulate(c, s_ref):
        rows = pl.ds(c * rc, rc)
        p = jnp.exp2(s_ref[...] - _lane_tile(m_s[rows, :], tk // LANES))
        acc_s[rows, :] = _lane_tile(alpha_s[rows, :], A_VEXT // LANES) * acc_s[rows, :] + jnp.dot(
            p.astype(BF16), v_ref[0], preferred_element_type=F32)

    bufs = (sa_s, sb_s)
    scores(0, bufs[0])
    for c in range(nrc):
        accumulate(c, bufs[c % 2])
        if c + 1 < nrc:
            scores(c + 1, bufs[(c + 1) % 2])

    @pl.when(kv == pl.num_programs(2) - 1)
    def _():
        acc = acc_s[...]
        l = jnp.max(acc[:, A_VDIM:], axis=-1, keepdims=True)
        o_ref[...] = (acc[:, :A_VDIM] / l * _silu(gate_ref[...].astype(F32))).astype(o_ref.dtype)


def _attention(q, k, v, p2, tq=8192, tk=2048):
    h, s, _ = q.shape
    tq, tk = min(tq, s), min(tk, s)
    rc = min(ATTN_ROWS, tq)
    return pl.pallas_call(
        _attn_kernel,
        grid=(h, s // tq, s // tk),
        in_specs=[
            pl.BlockSpec((1, tq, A_QK), lambda h, i, j: (h, i, 0)),
            pl.BlockSpec((1, tk, A_QK), lambda h, i, j: (h, j, 0)),
            pl.BlockSpec((1, tk, A_VEXT), lambda h, i, j: (h, j, 0)),
            pl.BlockSpec((tq, A_VDIM), lambda h, i, j: (i, P2_AGATE // A_VDIM + h)),
        ],
        out_specs=pl.BlockSpec((tq, A_VDIM), lambda h, i, j: (i, h)),
        out_shape=jax.ShapeDtypeStruct((s, A_HEADS * A_VDIM), BF16),
        scratch_shapes=[pltpu.VMEM((tq, LANES), F32), pltpu.VMEM((tq, LANES), F32), pltpu.VMEM((tq, A_VEXT), F32),
                        pltpu.VMEM((rc, tk), F32), pltpu.VMEM((rc, tk), F32)],
        compiler_params=_params(("parallel", "parallel", "arbitrary")),
        name="mla_attention",
    )(q, k, v, p2)


CONV_CW = 128
CONV_ROWS = 128
CONV_OFF = CONV_HALO - CONV_W // 2
CONV_REACH = (CONV_OFF + CONV_W - 1) // SUBLANES * SUBLANES


def _conv_kernel(a_ref, g_ref, ap_ref, gp_ref, an_ref, gn_ref, bg_ref, w_ref, cb_ref, lg_ref, lb_ref,
                 o_ref, u_s, c_s, sh_s):
    i = pl.program_id(0)
    tm = a_ref.shape[0]
    nchunk = BRANCH_W // CONV_CW
    glu = lambda a, g: a.astype(F32) * jax.nn.sigmoid(g.astype(F32))
    first, last = i > 0, i < pl.num_programs(0) - 1
    for ci in range(nchunk):
        cs = slice(ci * CONV_CW, (ci + 1) * CONV_CW)
        u_s[ci, CONV_HALO:CONV_HALO + tm, :] = glu(a_ref[:, cs], g_ref[:, cs])
        u_s[ci, :CONV_HALO, :] = jnp.where(first, glu(ap_ref[:, cs], gp_ref[:, cs]), 0.0)
        u_s[ci, CONV_HALO + tm:, :] = jnp.where(last, glu(an_ref[:, cs], gn_ref[:, cs]), 0.0)

    off, reach = CONV_OFF, CONV_REACH
    rows = min(CONV_ROWS, tm)

    def chunk(ci, carry):
        for r0 in range(0, tm, rows):
            acc = jnp.broadcast_to(cb_ref[ci], (rows, CONV_CW))
            for b in range(SUBLANES):
                sh_s[b] = u_s[ci, r0 + b:r0 + b + rows + reach, :]
                for t in range(CONV_W):
                    if (off + t) % SUBLANES == b:
                        a = off + t - b
                        acc = acc + sh_s[b, a:a + rows, :] * w_ref[ci, t:t + 1, :]
            c_s[ci, r0:r0 + rows, :] = acc
        return carry

    lax.fori_loop(0, nchunk, chunk, 0)

    total = c_s[0]
    for ci in range(1, nchunk):
        total = total + c_s[ci]
    mean = jnp.sum(total, axis=-1, keepdims=True) * (1.0 / BRANCH_W)
    sq = jnp.zeros_like(total)
    for ci in range(nchunk):
        d = c_s[ci] - mean
        sq = sq + d * d
    rstd = lax.rsqrt(jnp.sum(sq, axis=-1, keepdims=True) * (1.0 / BRANCH_W) + EPS)
    for ci in range(nchunk):
        cs = slice(ci * CONV_CW, (ci + 1) * CONV_CW)
        y = (c_s[ci] - mean) * rstd * lg_ref[:, cs] + lb_ref[:, cs]
        o_ref[:, cs] = (_silu(y) * _silu(bg_ref[:, cs].astype(F32))).astype(o_ref.dtype)


def _conv_module(p2, w, cb, lg, lb, tm=256):
    s = p2.shape[0]
    tm = min(tm, s)
    r = tm // CONV_HALO
    nh = s // CONV_HALO
    wide = lambda c: pl.BlockSpec((tm, BRANCH_W), lambda i, c=c: (i, c))
    prev = lambda c: pl.BlockSpec((CONV_HALO, BRANCH_W), lambda i, c=c: (jnp.maximum(i * r - 1, 0), c))
    nxt = lambda c: pl.BlockSpec((CONV_HALO, BRANCH_W), lambda i, c=c: (jnp.minimum((i + 1) * r, nh - 1), c))
    const = lambda rows: pl.BlockSpec((rows, BRANCH_W), lambda i: (0, 0))
    nchunk = BRANCH_W // CONV_CW
    chunked = lambda rows: pl.BlockSpec((nchunk, rows, CONV_CW), lambda i: (0, 0, 0))
    ca, cg, cbg = P2_GLU_A // BRANCH_W, P2_GLU_G // BRANCH_W, P2_BGATE // BRANCH_W
    w_c = jnp.transpose(w.reshape(w.shape[0], nchunk, CONV_CW), (1, 0, 2))
    cb_c = cb.reshape(nchunk, 1, CONV_CW)
    return pl.pallas_call(
        _conv_kernel,
        grid=(s // tm,),
        in_specs=[wide(ca), wide(cg), prev(ca), prev(cg), nxt(ca), nxt(cg), wide(cbg),
                  chunked(w.shape[0]), chunked(1), const(1), const(1)],
        out_specs=pl.BlockSpec((tm, BRANCH_W), lambda i: (i, 0)),
        out_shape=jax.ShapeDtypeStruct((s, BRANCH_W), BF16),
        scratch_shapes=[pltpu.VMEM((nchunk, tm + 2 * CONV_HALO, CONV_CW), F32),
                        pltpu.VMEM((nchunk, tm, CONV_CW), F32),
                        pltpu.VMEM((SUBLANES, min(CONV_ROWS, tm) + CONV_REACH, CONV_CW), F32)],
        compiler_params=_params(("parallel",)),
        name="conv_module",
    )(p2, p2, p2, p2, p2, p2, p2, w_c, cb_c, lg, lb)


def _mlstm_kernel(qf_ref, kf_ref, vf_ref, gf_ref, qb_ref, kb_ref, vb_ref, gb_ref, bias_ref,
                  hf_ref, hb_ref, c_s, n_s, m_s):
    L = qf_ref.shape[0]

    @pl.when(pl.program_id(0) == 0)
    def _():
        c_s[...] = jnp.zeros(c_s.shape, F32)
        n_s[...] = jnp.zeros(n_s.shape, F32)
        m_s[...] = jnp.zeros(m_s.shape, F32)

    rows = lax.broadcasted_iota(jnp.int32, (L, L), 0)
    cols = lax.broadcasted_iota(jnp.int32, (L, L), 1)
    lower = rows >= cols
    upper = rows <= cols
    tri_l = lower.astype(F32)
    tri_u = upper.astype(F32)
    hi = lax.Precision.HIGHEST

    for d, (q_ref, k_ref, v_ref, g_ref, h_ref) in enumerate(
            ((qf_ref, kf_ref, vf_ref, gf_ref, hf_ref), (qb_ref, kb_ref, vb_ref, gb_ref, hb_ref))):
        g = g_ref[...] + bias_ref[...]
        g_t = g.T
        lf = jax.nn.log_sigmoid(g)
        lf_t = jax.nn.log_sigmoid(g_t)
        mask = lower if d == 0 else upper
        bcum = jnp.dot(tri_l if d == 0 else tri_u, lf, precision=hi, preferred_element_type=F32)
        bcum_t = jnp.dot(lf_t, tri_u if d == 0 else tri_l, precision=hi, preferred_element_type=F32)
        last = L - 1 if d == 0 else 0
        for h in range(C_HEADS):
            ci, cf, sidx = d * 8 + h, d * 8 + 4 + h, d * C_HEADS + h
            qh = q_ref[:, h * C_DK:(h + 1) * C_DK]
            kh = k_ref[:, h * C_DK:(h + 1) * C_DK] * (C_DK ** -0.5)
            vh = v_ref[:, h * C_DV:(h + 1) * C_DV]
            bcol, brow = bcum[:, cf:cf + 1], bcum_t[cf:cf + 1, :]
            icol, irow = g[:, ci:ci + 1], g_t[ci:ci + 1, :]
            m_prev = m_s[sidx]
            c_prev = c_s[sidx]
            n_prev = n_s[sidx]

            logw = jnp.where(mask, bcol - brow + irow, -jnp.inf)
            inter = bcol + m_prev
            m_loc = jnp.maximum(inter, jnp.max(logw, axis=-1, keepdims=True))
            w_intra = jnp.exp(logw - m_loc)
            w_inter = jnp.exp(inter - m_loc)
            s = lax.dot_general(qh, kh, (((1,), (1,)), ((), ())), preferred_element_type=F32) * w_intra
            num = (jnp.dot(s.astype(BF16), vh, preferred_element_type=F32)
                   + w_inter * jnp.dot(qh, c_prev.astype(BF16), preferred_element_type=F32))
            den = (jnp.sum(s, axis=-1, keepdims=True)
                   + w_inter * jnp.sum(qh.astype(F32) * n_prev, axis=-1, keepdims=True))
            h_ref[:, h * C_DV:(h + 1) * C_DV] = num / jnp.maximum(jnp.abs(den), jnp.exp(-m_loc))

            b_tot = bcol[last:last + 1, :]
            lw_end = b_tot - bcol + icol
            m_new = jnp.maximum(b_tot + m_prev, jnp.max(lw_end, axis=0, keepdims=True))
            decay = jnp.exp(b_tot + m_prev - m_new)
            wk = jnp.exp(lw_end - m_new) * kh.astype(F32)
            c_s[sidx] = decay * c_prev + lax.dot_general(
                wk.astype(BF16), vh, (((0,), (0,)), ((), ())), preferred_element_type=F32)
            n_s[sidx] = decay * n_prev + jnp.sum(wk, axis=0, keepdims=True)
            m_s[sidx] = m_new


def _mlstm(p1, p2, bias):
    s = p2.shape[0]
    L = min(MLSTM_CHUNK, s)
    nc = s // L
    wq, wv = C_HEADS * C_DK, C_HEADS * C_DV
    fwd = lambda w, c: pl.BlockSpec((L, w), lambda i, c=c: (i, c))
    bwd = lambda w, c: pl.BlockSpec((L, w), lambda i, c=c: (nc - 1 - i, c))
    specs = lambda mk: [mk(wq, P2_CQ // wq), mk(wq, P2_CK // wq), mk(wv, P2_CV // wv)]
    return pl.pallas_call(
        _mlstm_kernel,
        grid=(nc,),
        in_specs=specs(fwd) + [fwd(LANES, P1_CIF // LANES)] + specs(bwd) + [bwd(LANES, P1_CIF // LANES)]
        + [pl.BlockSpec((1, LANES), lambda i: (0, 0))],
        out_specs=[pl.BlockSpec((L, wv), lambda i: (i, 0)), pl.BlockSpec((L, wv), lambda i: (nc - 1 - i, 0))],
        out_shape=[jax.ShapeDtypeStruct((s, wv), F32), jax.ShapeDtypeStruct((s, wv), F32)],
        scratch_shapes=[pltpu.VMEM((2 * C_HEADS, C_DK, C_DV), F32), pltpu.VMEM((2 * C_HEADS, 1, C_DK), F32),
                        pltpu.VMEM((2 * C_HEADS, 1, 1), F32)],
        compiler_params=_params(("arbitrary",)),
        name="mlstm_scan",
    )(p2, p2, p2, p1, p2, p2, p2, p1, bias)


def _mlstm_out_kernel(hf_ref, hb_ref, o_ref, gate_ref, ng_ref, y_ref):
    for h in range(C_HEADS):
        sl = slice(h * C_DV, (h + 1) * C_DV)
        x = hf_ref[:, sl] + hb_ref[:, sl]
        x = x * lax.rsqrt(jnp.mean(x * x, axis=-1, keepdims=True) + EPS) * ng_ref[:, sl]
        x = jax.nn.sigmoid(o_ref[:, sl].astype(F32)) * x
        y_ref[:, sl] = (x * _silu(gate_ref[:, sl].astype(F32))).astype(y_ref.dtype)


def _mlstm_out(hf, hb, p2, ng, tm=256):
    s, w = hf.shape
    tm = min(tm, s)
    row = lambda c: pl.BlockSpec((tm, w), lambda i, c=c: (i, c))
    return pl.pallas_call(
        _mlstm_out_kernel,
        grid=(s // tm,),
        in_specs=[row(0), row(0), row(P2_CO // w), row(P2_CGATE // w), pl.BlockSpec((1, w), lambda i: (0, 0))],
        out_specs=row(0),
        out_shape=jax.ShapeDtypeStruct((s, w), BF16),
        compiler_params=_params(("parallel",)),
        name="mlstm_out",
    )(hf, hb, p2, p2, ng)


def _mem_kv_kernel(mem_ref, g_ref, w_ref, o_ref):
    hn = _rms(mem_ref[...], g_ref[...]).astype(BF16)
    o_ref[...] = jnp.dot(hn, w_ref[...], preferred_element_type=F32).astype(o_ref.dtype)


def _mem_kv(mem, g, w):
    nm, d = mem.shape
    n = w.shape[1]
    tn = 512
    return pl.pallas_call(
        _mem_kv_kernel,
        grid=(n // tn,),
        in_specs=[pl.BlockSpec((nm, d), lambda j: (0, 0)), pl.BlockSpec((1, d), lambda j: (0, 0)),
                  pl.BlockSpec((d, tn), lambda j: (0, j))],
        out_specs=pl.BlockSpec((nm, tn), lambda j: (0, j)),
        out_shape=jax.ShapeDtypeStruct((nm, n), BF16),
        compiler_params=_params(("parallel",)),
        name="mem_kv",
    )(mem, g, w)


def _mem_attn_kernel(q_ref, kv_ref, o_ref):
    scale = D_HDIM ** -0.5
    for h in range(D_HEADS):
        sl = slice(h * D_HDIM, (h + 1) * D_HDIM)
        q = (q_ref[:, sl] * scale).astype(BF16)
        s = lax.dot_general(q, kv_ref[:, sl], (((1,), (1,)), ((), ())), preferred_element_type=F32)
        p = jnp.exp(s - jnp.max(s, axis=-1, keepdims=True))
        l = jnp.sum(p, axis=-1, keepdims=True)
        v = kv_ref[:, D_W + h * D_HDIM:D_W + (h + 1) * D_HDIM]
        o_ref[:, sl] = (jnp.dot(p.astype(BF16), v, preferred_element_type=F32) / l).astype(o_ref.dtype)


def _mem_attention(p1, kv, tm=512):
    s = p1.shape[0]
    tm = min(tm, s)
    nm = kv.shape[0]
    return pl.pallas_call(
        _mem_attn_kernel,
        grid=(s // tm,),
        in_specs=[pl.BlockSpec((tm, D_W), lambda i: (i, P1_DQ // D_W)),
                  pl.BlockSpec((nm, 2 * D_W), lambda i: (0, 0))],
        out_specs=pl.BlockSpec((tm, D_W), lambda i: (i, 0)),
        out_shape=jax.ShapeDtypeStruct((s, D_W), BF16),
        compiler_params=_params(("parallel",)),
        name="mem_attention",
    )(p1, kv)


def _merge_kernel(h_ref, ya_ref, yb_ref, yc_ref, yd_ref, wm0, wm1, wm2, wm3, wa, wb, wc, wd, z_ref):
    h = h_ref[...]
    z = None
    for y_ref, wm, wp in ((ya_ref, wm0, wa), (yb_ref, wm1, wb), (yc_ref, wm2, wc), (yd_ref, wm3, wd)):
        gate = jax.nn.sigmoid(lax.dot_general(h, wm[...], NT_DIMS, preferred_element_type=F32))
        term = gate * jnp.dot(y_ref[...], wp[...].astype(BF16), preferred_element_type=F32)
        z = term if z is None else z + term
    z_ref[...] = z.astype(z_ref.dtype)


def _merge(h, ya, yb, yc, yd, wm_t, wa, wb, wc, wd, tm=512, tn=256):
    s, d = h.shape
    tm = min(tm, s)
    nj = d // tn
    row = lambda w: pl.BlockSpec((tm, w), lambda j, i: (i, 0))
    col = lambda k: pl.BlockSpec((k, tn), lambda j, i: (0, j))
    gate_w = lambda b: pl.BlockSpec((tn, d), lambda j, i, b=b: (b * nj + j, 0))
    return pl.pallas_call(
        _merge_kernel,
        grid=(nj, s // tm),
        in_specs=[row(d), row(ya.shape[1]), row(yb.shape[1]), row(yc.shape[1]), row(yd.shape[1]),
                  gate_w(0), gate_w(1), gate_w(2), gate_w(3),
                  col(wa.shape[0]), col(wb.shape[0]), col(wc.shape[0]), col(wd.shape[0])],
        out_specs=pl.BlockSpec((tm, tn), lambda j, i: (i, j)),
        out_shape=jax.ShapeDtypeStruct((s, d), BF16),
        compiler_params=_params(("parallel", "parallel")),
        name="gated_merge",
    )(h, ya, yb, yc, yd, wm_t, wm_t, wm_t, wm_t, wa, wb, wc, wd)


def _pack_in_proj(w_t, layer):
    d = w_t.shape[2]
    rows = lambda a, b: lax.slice(w_t, (layer, a, 0), (layer + 1, b, d))[0]
    kr = rows(P1_CKV + A_KV_RANK, IN_AGATE)
    half = A_ROPE // 2
    kr_rot = jnp.concatenate([-kr[half:], kr[:half]], axis=0)
    zeros = lambda n: jnp.zeros((n, d), w_t.dtype)
    w1 = jnp.concatenate([
        rows(0, P1_CKV + A_KV_RANK), rows(IN_DQ, IN_MERGE),
        kr, zeros(LANES - A_ROPE), kr_rot, zeros(LANES - A_ROPE),
        rows(IN_CIF, IN_DQ), zeros(P1_W - P1_CIF - (IN_DQ - IN_CIF))], axis=0)
    w1 = _cast_rows(w1, 0, P1_W)
    n_in = w_t.shape[1]
    w_flat = w_t.reshape(-1, d)
    w2 = _cast_rows(w_flat, layer * n_in + IN_AGATE, IN_CIF - IN_AGATE)
    wm = _cast_rows(w_flat, layer * n_in + IN_MERGE, n_in - IN_MERGE)
    return w1, w2, wm


def _pack_uq(w_uq):
    r = w_uq.shape[0]
    w = w_uq.reshape(r, A_HEADS, A_NOPE + A_ROPE)
    nope, rope = w[..., :A_NOPE], w[..., A_NOPE:]
    half = A_ROPE // 2
    rot = jnp.concatenate([-rope[..., half:], rope[..., :half]], axis=-1)
    packed = jnp.concatenate([nope, rope, rot], axis=-1)
    return jnp.transpose(packed, (1, 0, 2)).astype(BF16)


def _pack_ukv(w_ukv):
    r = w_ukv.shape[0]
    w = w_ukv.reshape(r, A_HEADS, A_NOPE + A_VDIM)
    return jnp.transpose(w, (1, 0, 2)).astype(BF16)


def _layer(x, mem, pos, invf, norm_g, w_in, layer, cq_g, w_uq, ckv_g, w_ukv, conv_w, conv_b, ln_g, ln_b,
           gate_b, mlstm_g, mem_g, w_mem_kv, w_pa, w_pb, w_pc, w_pd, w_out):
    w1, w2, wm = _pack_in_proj(w_in, layer)
    h = _rmsnorm(x, norm_g, BF16)
    p1 = _matmul(h, w1, F32, tm=1024, tn=1280, transposed_w=True, name="in_proj_f32")
    p2 = _matmul(h, w2, BF16, tm=1024, tn=1024, transposed_w=True, name="in_proj_bf16")

    q_gain = cq_g * ((A_NOPE + A_ROPE) ** -0.5 * LOG2_E)
    q, k, v = _mla_prep(p1, pos, invf, q_gain.reshape(1, -1), ckv_g.reshape(1, -1), _pack_uq(w_uq), _pack_ukv(w_ukv))
    y_a = _attention(q, k, v, p2)

    conv_w_pad = jnp.concatenate([conv_w, jnp.zeros((1, BRANCH_W), F32)], axis=0)
    y_b = _conv_module(p2, conv_w_pad, conv_b.reshape(1, -1), ln_g.reshape(1, -1), ln_b.reshape(1, -1))

    bias = jnp.zeros((1, LANES), F32).at[0, :4 * C_HEADS].set(gate_b.reshape(-1))
    hf, hb = _mlstm(p1, p2, bias)
    y_c = _mlstm_out(hf, hb, p2, mlstm_g.reshape(1, -1))

    kv = _mem_kv(mem, mem_g.reshape(1, -1), w_mem_kv.astype(BF16))
    y_d = _mem_attention(p1, kv)

    z = _merge(h, y_a, y_b, y_c, y_d, wm, w_pa, w_pb, w_pc, w_pd)
    return _matmul(z, w_out.astype(BF16), F32, tm=1024, tn=1024, residual=x, name="out_proj")


def kernel(x, mem, positions, norm_g, w_in, mla_cq_norm_g, w_uq, mla_ckv_norm_g, w_ukv, conv_w, conv_b,
           conv_ln_g, conv_ln_b, mlstm_gate_b, mlstm_norm_g, mem_norm_g, w_mem_kv, w_proj_a, w_proj_b,
           w_proj_c, w_proj_d, w_out, final_norm_g):
    b, s, d = x.shape
    depth = w_in.shape[0]
    inv_freq = ROPE_THETA ** (-jnp.arange(0, A_ROPE, 2, dtype=F32) / A_ROPE)
    invf = jnp.tile(inv_freq, LANES // (A_ROPE // 2)).reshape(1, LANES)
    w_in = jnp.swapaxes(w_in, 1, 2)
    outs = []
    for bi in range(b):
        xb = x[bi]
        pos = positions[bi].astype(F32).reshape(s, 1)
        for l in range(depth):
            xb = _layer(xb, mem[bi], pos, invf, norm_g[l], w_in, l, mla_cq_norm_g[l], w_uq[l],
                        mla_ckv_norm_g[l], w_ukv[l], conv_w[l], conv_b[l], conv_ln_g[l], conv_ln_b[l],
                        mlstm_gate_b[l], mlstm_norm_g[l], mem_norm_g[l], w_mem_kv[l], w_proj_a[l],
                        w_proj_b[l], w_proj_c[l], w_proj_d[l], w_out[l])
        outs.append(_rmsnorm(xb, final_norm_g, x.dtype))
    return jnp.stack(outs, axis=0)
```
